```python
import math
import jax, jax.numpy as jnp
from jax import lax
import numpy as np

D_MODEL = 2048
BATCH = 4
SEQ = 4096
DEPTH = 4

A_HEAD_DIM = 128
A_HEADS = D_MODEL // (2 * A_HEAD_DIM)
A_WIDTH = A_HEADS * A_HEAD_DIM
MOBA_BLOCK = 256
MOBA_TOPK = 3
MOBA_Q_CHUNK = 32

B_HEAD_DIM = 64
B_WIDTH = D_MODEL - A_WIDTH
B_HEADS = B_WIDTH // B_HEAD_DIM
LORA_W = max(32, int(round(math.sqrt(B_WIDTH) * 1.8 / 32)) * 32)
LORA_A = max(32, int(round(math.sqrt(B_WIDTH) * 1.8 / 32)) * 32)
LORA_G = max(32, int(round(B_WIDTH ** 0.6 * 0.8 / 32)) * 32)
B_PROJ = 3 * B_WIDTH + LORA_W + LORA_A + LORA_G
AB_IN = 3 * A_WIDTH + B_PROJ
MIX_WIDTH = A_WIDTH + B_WIDTH
GN_EPS = 64e-5

POOL_WINDOWS = (2, 4, 8, 16)
POOL_GROUP = D_MODEL // len(POOL_WINDOWS)

D_FF = ((8 * D_MODEL // 3 + 255) // 256) * 256
RMS_EPS = 1e-6
N_EVEN = (DEPTH + 1) // 2
N_ODD = DEPTH // 2

kernel_name = "hybrid_moba_rwkv7_pool_macaron"


def rms_norm(x, g):
    xf = x.astype(jnp.float32)
    y = xf * lax.rsqrt(jnp.mean(xf * xf, axis=-1, keepdims=True) + RMS_EPS)
    return (y * g.astype(jnp.float32)).astype(x.dtype)


def swiglu(x, wg, wu, wd):
    return (jax.nn.silu(x @ wg) * (x @ wu)) @ wd


def token_shift(z):
    return jnp.pad(z, ((0, 0), (1, 0), (0, 0)))[:, :-1]


def moba_attention(q, k, v):
    b, s, h, dh = q.shape
    nb = -(-s // MOBA_BLOCK)
    pad = nb * MOBA_BLOCK - s
    qh = q.transpose(0, 2, 1, 3)
    kb = jnp.pad(k.transpose(0, 2, 1, 3), ((0, 0), (0, 0), (0, pad), (0, 0))).reshape(b, h, nb, MOBA_BLOCK, dh)
    vb = jnp.pad(v.transpose(0, 2, 1, 3), ((0, 0), (0, 0), (0, pad), (0, 0))).reshape(b, h, nb, MOBA_BLOCK, dh)
    k_mean = jnp.mean(kb.astype(jnp.float32), axis=3)
    topk = min(MOBA_TOPK, nb)
    scale = dh ** -0.5
    n_chunks = s // MOBA_Q_CHUNK
    bi = jnp.arange(b)[:, None, None, None]
    hi = jnp.arange(h)[None, :, None, None]

    def chunk(c):
        start = c * MOBA_Q_CHUNK
        blk = start // MOBA_BLOCK
        qc = lax.dynamic_slice_in_dim(qh, start, MOBA_Q_CHUNK, axis=2).astype(jnp.float32)
        pos = start + jnp.arange(MOBA_Q_CHUNK)
        gate = jnp.einsum('bhqd,bhnd->bhqn', qc, k_mean)
        gate = jnp.where(jnp.arange(nb) < blk, gate, -jnp.inf)
        _, sel = lax.top_k(gate, topk)
        sel_valid = sel < blk
        k_sel = kb[bi, hi, sel].astype(jnp.float32)
        v_sel = vb[bi, hi, sel].astype(jnp.float32)
        s_sel = jnp.einsum('bhqd,bhqnkd->bhqnk', qc, k_sel) * scale
        s_sel = jnp.where(sel_valid[..., None], s_sel, -jnp.inf).reshape(b, h, MOBA_Q_CHUNK, topk * MOBA_BLOCK)
        k_own = lax.dynamic_index_in_dim(kb, blk, axis=2, keepdims=False).astype(jnp.float32)
        v_own = lax.dynamic_index_in_dim(vb, blk, axis=2, keepdims=False).astype(jnp.float32)
        s_own = jnp.einsum('bhqd,bhkd->bhqk', qc, k_own) * scale
        key_pos = blk * MOBA_BLOCK + jnp.arange(MOBA_BLOCK)
        s_own = jnp.where(key_pos[None, :] <= pos[:, None], s_own, -jnp.inf)
        p = jax.nn.softmax(jnp.concatenate([s_own, s_sel], axis=-1), axis=-1)
        p_own = p[..., :MOBA_BLOCK]
        p_sel = p[..., MOBA_BLOCK:].reshape(b, h, MOBA_Q_CHUNK, topk, MOBA_BLOCK)
        out = jnp.einsum('bhqk,bhkd->bhqd', p_own, v_own) + jnp.einsum('bhqnk,bhqnkd->bhqd', p_sel, v_sel)
        return out.astype(q.dtype)

    out = lax.map(chunk, jnp.arange(n_chunks))
    return out.transpose(1, 0, 3, 2, 4).reshape(b, s, h * dh)


def rwkv7_time_mix(zb, mu, w0, w2, a0, a2, g2, k_k, k_a, r_k, gn_g, gn_b):
    b, s, _ = zb.shape
    zf = zb.astype(jnp.float32)
    zf = zf + (token_shift(zf) - zf) * mu
    r, k, v, w_lo, a_lo, g_lo = jnp.split(
        zf, [B_WIDTH, 2 * B_WIDTH, 3 * B_WIDTH, 3 * B_WIDTH + LORA_W, 3 * B_WIDTH + LORA_W + LORA_A], axis=-1)
    w = -jax.nn.softplus(-(w0 + jnp.tanh(w_lo) @ w2)) - 0.5
    decay = jnp.exp(-jnp.exp(w))
    a = jax.nn.sigmoid(a0 + a_lo @ a2)
    g = jax.nn.sigmoid(g_lo) @ g2

    def heads(t):
        return t.reshape(b, s, B_HEADS, B_HEAD_DIM)

    kk = heads(k * k_k)
    kk = kk * lax.rsqrt(jnp.maximum(jnp.sum(kk * kk, axis=-1, keepdims=True), 1e-24))
    k = k * (1.0 + (a - 1.0) * k_a)
    r_h, k_h, v_h, w_h, a_h = heads(r), heads(k), heads(v), heads(decay), heads(a)

    def step(state, inp):
        r_t, w_t, k_t, v_t, kk_t, a_t = inp
        sa = jnp.einsum('bhvk,bhk->bhv', state, -kk_t)
        state = (state * w_t[:, :, None, :] + sa[..., None] * (kk_t * a_t)[:, :, None, :]
                 + v_t[..., None] * k_t[:, :, None, :])
        return state, jnp.einsum('bhvk,bhk->bhv', state, r_t)

    xs = (jnp.moveaxis(r_h, 1, 0), jnp.moveaxis(w_h, 1, 0), jnp.moveaxis(k_h, 1, 0),
          jnp.moveaxis(v_h, 1, 0), jnp.moveaxis(kk, 1, 0), jnp.moveaxis(a_h, 1, 0))
    state0 = jnp.zeros((b, B_HEADS, B_HEAD_DIM, B_HEAD_DIM), jnp.float32)
    _, y = lax.scan(step, state0, xs)
    y = jnp.moveaxis(y, 0, 1)
    mean = jnp.mean(y, axis=-1, keepdims=True)
    var = jnp.mean(jnp.square(y - mean), axis=-1, keepdims=True)
    y = (y - mean) * lax.rsqrt(var + GN_EPS) * gn_g.reshape(B_HEADS, B_HEAD_DIM) + gn_b.reshape(B_HEADS, B_HEAD_DIM)
    y = y + jnp.sum(r_h * k_h * r_k, axis=-1, keepdims=True) * v_h
    return (y.reshape(b, s, B_WIDTH) * g).astype(zb.dtype)


def moba_rwkv_mixer(u, w_in, w_out, mu, w0, w2, a0, a2, g2, k_k, k_a, r_k, gn_g, gn_b):
    b, s, _ = u.shape
    z = u @ w_in
    qa, ka, va, zb = jnp.split(z, [A_WIDTH, 2 * A_WIDTH, 3 * A_WIDTH], axis=-1)
    hd = (b, s, A_HEADS, A_HEAD_DIM)
    y_a = moba_attention(qa.reshape(hd), ka.reshape(hd), va.reshape(hd))
    y_b = rwkv7_time_mix(zb, mu, w0, w2, a0, a2, g2, k_k, k_a, r_k, gn_g, gn_b)
    return jnp.concatenate([y_a, y_b], axis=-1) @ w_out


def pool_mixer(u, w_grp, scale):
    b, s, d = u.shape
    uf = u.astype(jnp.float32)
    cs = jnp.concatenate([jnp.zeros((b, 1, d), jnp.float32), lax.cumsum(uf, axis=1)], axis=1)
    t = jnp.arange(s)
    outs = []
    for gi, win in enumerate(POOL_WINDOWS):
        lo_c, hi_c = gi * POOL_GROUP, (gi + 1) * POOL_GROUP
        csg = cs[:, :, lo_c:hi_c]
        lo = jnp.maximum(t + 1 - win, 0)
        cnt = jnp.minimum(t + 1, win).astype(jnp.float32)[:, None]
        mean = (csg[:, 1:] - csg[:, lo]) / cnt
        outs.append((mean - uf[:, :, lo_c:hi_c]).astype(u.dtype) @ w_grp[gi])
    return jnp.concatenate(outs, axis=-1) * scale


def setup_inputs(seed: int = 0) -> dict:
    key = jax.random.key(seed)
    ks = iter(jax.random.split(key, 32))

    def nrm(shape, scale):
        return jax.random.normal(next(ks), shape, jnp.float32) * scale

    d, f = D_MODEL, D_FF
    return {
        'x': nrm((BATCH, SEQ, d), 1.0),
        'ffn1_norm': 1.0 + nrm((DEPTH, d), 0.05),
        'ffn1_wg': nrm((DEPTH, d, f), d ** -0.5),
        'ffn1_wu': nrm((DEPTH, d, f), d ** -0.5),
        'ffn1_wd': nrm((DEPTH, f, d), f ** -0.5),
        'mix_norm': 1.0 + nrm((DEPTH, d), 0.05),
        'ffn2_norm': 1.0 + nrm((DEPTH, d), 0.05),
        'ffn2_wg': nrm((DEPTH, d, f), d ** -0.5),
        'ffn2_wu': nrm((DEPTH, d, f), d ** -0.5),
        'ffn2_wd': nrm((DEPTH, f, d), f ** -0.5),
        'ab_w_in': nrm((N_EVEN, d, AB_IN), d ** -0.5),
        'ab_w_out': nrm((N_EVEN, MIX_WIDTH, d), MIX_WIDTH ** -0.5),
        'rwkv_mu': jax.random.uniform(next(ks), (N_EVEN, B_PROJ), jnp.float32),
        'rwkv_w0': jax.random.uniform(next(ks), (N_EVEN, B_WIDTH), jnp.float32, -6.5, -1.5),
        'rwkv_w2': nrm((N_EVEN, LORA_W, B_WIDTH), 0.1 * LORA_W ** -0.5),
        'rwkv_a0': nrm((N_EVEN, B_WIDTH), 0.1),
        'rwkv_a2': nrm((N_EVEN, LORA_A, B_WIDTH), LORA_A ** -0.5),
        'rwkv_g2': nrm((N_EVEN, LORA_G, B_WIDTH), LORA_G ** -0.5),
        'rwkv_k_k': 0.85 + nrm((N_EVEN, B_WIDTH), 0.05),
        'rwkv_k_a': 1.0 + nrm((N_EVEN, B_WIDTH), 0.05),
        'rwkv_r_k': nrm((N_EVEN, B_HEADS, B_HEAD_DIM), 0.1),
        'rwkv_gn_g': 1.0 + nrm((N_EVEN, B_WIDTH), 0.05),
        'rwkv_gn_b': nrm((N_EVEN, B_WIDTH), 0.01),
        'pool_w': nrm((N_ODD, len(POOL_WINDOWS), POOL_GROUP, POOL_GROUP), POOL_GROUP ** -0.5),
        'pool_scale': 1.0 + nrm((N_ODD, d), 0.1),
        'final_norm': 1.0 + nrm((d,), 0.05),
    }


def reference(x, ffn1_norm, ffn1_wg, ffn1_wu, ffn1_wd, mix_norm, ffn2_norm, ffn2_wg, ffn2_wu, ffn2_wd,
              ab_w_in, ab_w_out, rwkv_mu, rwkv_w0, rwkv_w2, rwkv_a0, rwkv_a2, rwkv_g2, rwkv_k_k, rwkv_k_a,
              rwkv_r_k, rwkv_gn_g, rwkv_gn_b, pool_w, pool_scale, final_norm):
    h = x
    for layer in range(DEPTH):
        h = h + 0.5 * swiglu(rms_norm(h, ffn1_norm[layer]), ffn1_wg[layer], ffn1_wu[layer], ffn1_wd[layer])
        u = rms_norm(h, mix_norm[layer])
        if layer % 2 == 0:
            e = layer // 2
            h = h + moba_rwkv_mixer(u, ab_w_in[e], ab_w_out[e], rwkv_mu[e], rwkv_w0[e], rwkv_w2[e],
                                    rwkv_a0[e], rwkv_a2[e], rwkv_g2[e], rwkv_k_k[e], rwkv_k_a[e],
                                    rwkv_r_k[e], rwkv_gn_g[e], rwkv_gn_b[e]).astype(h.dtype)
        else:
            o = layer // 2
            h = h + pool_mixer(u, pool_w[o], pool_scale[o]).astype(h.dtype)
        h = h + 0.5 * swiglu(rms_norm(h, ffn2_norm[layer]), ffn2_wg[layer], ffn2_wu[layer], ffn2_wd[layer])
    return rms_norm(h, final_norm)
```

```python
import functools
import math

import jax
import jax.numpy as jnp
from jax import lax
from jax.experimental import pallas as pl
from jax.experimental.pallas import tpu as pltpu

F32 = jnp.float32
BF16 = jnp.bfloat16

A_HEAD_DIM = 128
MOBA_BLOCK = 256
MOBA_TOPK = 3
B_HEAD_DIM = 64
LORA = 64
GN_EPS = 64e-5
RMS_EPS = 1e-6
POOL_WINDOWS = (2, 4, 8, 16)

LANES = 128
SUBLANES = 8
VMEM_LIMIT_BYTES = 56 * 1024 * 1024

NEG_BIG = -1e30
RWKV_CHUNK = 64

_NT = (((1,), (1,)), ((), ()))


def _rms(x, gain):
    ms = jnp.mean(x * x, axis=-1, keepdims=True)
    return x * lax.rsqrt(ms + RMS_EPS) * gain


def _params(*sem):
    return pltpu.CompilerParams(dimension_semantics=sem, vmem_limit_bytes=VMEM_LIMIT_BYTES)


def _ffn_body(h_ref, g_ref, wg_ref, wu_ref, wd_ref, fg_ref, o_ref, n_ref, acc_ref, *, final):
    f = pl.program_id(1)

    @pl.when(f == 0)
    def _():
        n_ref[...] = _rms(h_ref[...], g_ref[...]).astype(BF16)
        acc_ref[...] = jnp.zeros_like(acc_ref)

    n = n_ref[...]
    gate = jnp.dot(n, wg_ref[...], preferred_element_type=F32)
    up = jnp.dot(n, wu_ref[...], preferred_element_type=F32)
    act = (gate * jax.nn.sigmoid(gate) * up).astype(BF16)
    acc_ref[...] += jnp.dot(act, wd_ref[...], preferred_element_type=F32)

    @pl.when(f == pl.num_programs(1) - 1)
    def _():
        out = h_ref[...] + 0.5 * acc_ref[...]
        if final:
            out = _rms(out, fg_ref[...])
        o_ref[...] = out


def _ffn(h, gain, wg, wu, wd, layer, final_gain, *, final, tm, tf):
    t, d = h.shape
    f = wg.shape[-1]
    return pl.pallas_call(
        functools.partial(_ffn_body, final=final),
        grid=(t // tm, f // tf),
        in_specs=[
            pl.BlockSpec((tm, d), lambda i, j: (i, 0)),
            pl.BlockSpec((1, d), lambda i, j: (0, 0)),
            pl.BlockSpec((None, d, tf), lambda i, j: (layer, 0, j)),
            pl.BlockSpec((None, d, tf), lambda i, j: (layer, 0, j)),
            pl.BlockSpec((None, tf, d), lambda i, j: (layer, j, 0)),
            pl.BlockSpec((1, d), lambda i, j: (0, 0)),
        ],
        out_specs=pl.BlockSpec((tm, d), lambda i, j: (i, 0)),
        out_shape=jax.ShapeDtypeStruct((t, d), F32),
        scratch_shapes=[pltpu.VMEM((tm, d), BF16), pltpu.VMEM((tm, d), F32)],
        compiler_params=_params("parallel", "arbitrary"),
        name="ffn",
    )(h, gain, wg, wu, wd, final_gain)


def _inproj_body(h_ref, g_ref, w_ref, z_ref, n_ref):
    @pl.when(pl.program_id(1) == 0)
    def _():
        n_ref[...] = _rms(h_ref[...], g_ref[...]).astype(BF16)

    z_ref[...] = jnp.dot(n_ref[...], w_ref[...], preferred_element_type=F32)


def _inproj(h, gain, w_in, e, *, tm, tn):
    t, d = h.shape
    n = w_in.shape[-1]
    return pl.pallas_call(
        _inproj_body,
        grid=(t // tm, n // tn),
        in_specs=[
            pl.BlockSpec((tm, d), lambda i, j: (i, 0)),
            pl.BlockSpec((1, d), lambda i, j: (0, 0)),
            pl.BlockSpec((None, d, tn), lambda i, j: (e, 0, j)),
        ],
        out_specs=pl.BlockSpec((tm, tn), lambda i, j: (i, j)),
        out_shape=jax.ShapeDtypeStruct((t, n), F32),
        scratch_shapes=[pltpu.VMEM((tm, d), BF16)],
        compiler_params=_params("parallel", "arbitrary"),
        name="inproj",
    )(h, gain, w_in)


def _moba_body(q_ref, k_ref, v_ref, o_ref, kb_ref, vb_ref, km_ref, *, nb, blk, topk, scale):
    i = pl.program_id(2)
    dh = q_ref.shape[-1]

    @pl.when(i == 0)
    def _():
        kf = k_ref[0]
        kb_ref[...] = kf.astype(BF16)
        vb_ref[...] = v_ref[0].astype(BF16)
        km_ref[...] = jnp.zeros_like(km_ref)
        km_ref[0:nb, :] = jnp.mean(kf.reshape(nb, blk, dh), axis=1)

    q = q_ref[0]
    gate = lax.dot_general(q, km_ref[...], _NT, precision=lax.Precision.HIGHEST,
                           preferred_element_type=F32)
    lane = lax.broadcasted_iota(jnp.int32, (blk, LANES), 1)
    rank = jnp.zeros((blk, LANES), jnp.int32)
    for jp in range(nb):
        col = gate[:, jp:jp + 1]
        beats = (col > gate) | ((col == gate) & (jp < lane))
        rank = rank + jnp.where(beats & (jp < i), 1, 0)
    sel_bias = jnp.where((lane < i) & (rank < topk), 0.0, NEG_BIG)

    qb = q.astype(BF16)

    def scores(j):
        kj = kb_ref[pl.ds(pl.multiple_of(j * blk, blk), blk), :]
        return lax.dot_general(qb, kj, _NT, preferred_element_type=F32) * scale

    def pv(p, j):
        vj = vb_ref[pl.ds(pl.multiple_of(j * blk, blk), blk), :]
        return jnp.dot(p.astype(BF16), vj, preferred_element_type=F32)

    qpos = lax.broadcasted_iota(jnp.int32, (blk, blk), 0)
    kpos = lax.broadcasted_iota(jnp.int32, (blk, blk), 1)
    s = jnp.where(kpos <= qpos, scores(i), NEG_BIG)
    m0 = jnp.max(s, axis=-1, keepdims=True)
    p = jnp.exp(s - m0)
    l0 = jnp.sum(p, axis=-1, keepdims=True)
    acc0 = pv(p, i)

    def past(j, carry):
        m, l, acc = carry
        bias = jnp.max(jnp.where(lane == j, sel_bias, NEG_BIG), axis=-1, keepdims=True)
        s = scores(j) + bias
        m_new = jnp.maximum(m, jnp.max(s, axis=-1, keepdims=True))
        alpha = jnp.exp(m - m_new)
        p = jnp.exp(s - m_new)
        l = alpha * l + jnp.sum(p, axis=-1, keepdims=True)
        acc = alpha * acc + pv(p, j)
        return m_new, l, acc

    _, l, acc = lax.fori_loop(0, i, past, (m0, l0, acc0))
    o_ref[0] = acc / l


def _moba(z, *, heads):
    b, s, _ = z.shape
    dh, blk = A_HEAD_DIM, MOBA_BLOCK
    nb = s // blk
    assert s % blk == 0 and nb <= LANES
    return pl.pallas_call(
        functools.partial(_moba_body, nb=nb, blk=blk, topk=min(MOBA_TOPK, nb), scale=dh ** -0.5),
        grid=(b, heads, nb),
        in_specs=[
            pl.BlockSpec((1, blk, dh), lambda bi, h, i: (bi, i, h)),
            pl.BlockSpec((1, s, dh), lambda bi, h, i: (bi, 0, heads + h)),
            pl.BlockSpec((1, s, dh), lambda bi, h, i: (bi, 0, 2 * heads + h)),
        ],
        out_specs=pl.BlockSpec((1, blk, dh), lambda bi, h, i: (bi, i, h)),
        out_shape=jax.ShapeDtypeStruct((b, s, heads * dh), F32),
        scratch_shapes=[pltpu.VMEM((s, dh), BF16), pltpu.VMEM((s, dh), BF16),
                        pltpu.VMEM((LANES, dh), F32)],
        compiler_params=_params("parallel", "parallel", "arbitrary"),
        name="moba",
    )(z, z, z)


def _split_dot(x, ones_bd):
    hi = x.astype(BF16)
    lo = (x - hi.astype(F32)).astype(BF16)
    return (jnp.dot(hi, ones_bd, preferred_element_type=F32)
            + jnp.dot(lo, ones_bd, preferred_element_type=F32))


def _head_sum(x, ones_bd):
    cols = [_split_dot(x[:, c:c + LANES], ones_bd) for c in range(0, x.shape[-1], LANES)]
    return cols[0] if len(cols) == 1 else jnp.concatenate(cols, axis=-1)


def _head_ones():
    r = lax.broadcasted_iota(jnp.int32, (LANES, LANES), 0) // B_HEAD_DIM
    c = lax.broadcasted_iota(jnp.int32, (LANES, LANES), 1) // B_HEAD_DIM
    return jnp.where(r == c, 1.0, 0.0).astype(BF16)


def _prep_body(zr_ref, zk_ref, zv_ref, zl_ref, pr_ref, pk_ref, pv_ref, plo_ref,
               mur_ref, muk_ref, muv_ref, mul_ref, w0_ref, a0_ref, kk_ref, ka_ref, rk_ref,
               w2_ref, a2_ref, g2_ref,
               r_out, k_out, v_out, lw_out, kk_out, a_out, bonus_out, g_out):
    ts = zr_ref.shape[1]
    first = pl.program_id(1) == 0
    row = lax.broadcasted_iota(jnp.int32, (ts, 1), 0)

    def lerp(z_ref, prev_ref, mu_ref):
        x = z_ref[0]
        prev = jnp.where(first, 0.0, prev_ref[0, SUBLANES - 1:SUBLANES, :])
        shifted = jnp.where(row == 0, prev, pltpu.roll(x, 1, axis=0))
        return x + (shifted - x) * mu_ref[...]

    r = lerp(zr_ref, pr_ref, mur_ref)
    k = lerp(zk_ref, pk_ref, muk_ref)
    v = lerp(zv_ref, pv_ref, muv_ref)
    lo = lerp(zl_ref, plo_ref, mul_ref)
    wa = lo[:, 0:LANES]
    gl = lo[:, LANES:2 * LANES]

    w_in = w0_ref[...] + jnp.dot(jnp.tanh(wa).astype(BF16), w2_ref[...], preferred_element_type=F32)
    sp = jnp.maximum(-w_in, 0.0) + jnp.log(1.0 + jnp.exp(-jnp.abs(w_in)))
    w = -sp - 0.5
    lw = -jnp.exp(w)
    a = jax.nn.sigmoid(a0_ref[...] + jnp.dot(wa.astype(BF16), a2_ref[...], preferred_element_type=F32))
    g = jnp.dot(jax.nn.sigmoid(gl).astype(BF16), g2_ref[...], preferred_element_type=F32)

    ones_bd = _head_ones()
    kk = k * kk_ref[...]
    kk = kk * lax.rsqrt(jnp.maximum(_head_sum(kk * kk, ones_bd), 1e-24))
    k2 = k * (1.0 + (a - 1.0) * ka_ref[...])
    bonus = _head_sum(r * k2 * rk_ref[...], ones_bd) * v

    r_out[0] = r
    k_out[0] = k2
    v_out[0] = v
    lw_out[0] = lw
    kk_out[0] = kk
    a_out[0] = a
    bonus_out[0] = bonus
    g_out[0] = g


def _rwkv_prep(z, p, *, zb_col, ts):
    b, s, _ = z.shape
    bw = p["w0"].shape[-1]
    lw = 2 * LANES
    cb = zb_col // bw
    lb = (zb_col + 3 * bw) // lw
    assert zb_col % bw == 0 and (zb_col + 3 * bw) % lw == 0 and s % ts == 0 and ts % SUBLANES == 0
    sub = ts // SUBLANES

    def cur(col, width):
        return pl.BlockSpec((1, ts, width), lambda bi, t: (bi, t, col))

    def prev(col, width):
        return pl.BlockSpec((1, SUBLANES, width), lambda bi, t: (bi, jnp.maximum(t * sub - 1, 0), col))

    def vec(width):
        return pl.BlockSpec((1, width), lambda bi, t: (0, 0))

    def mat(width):
        return pl.BlockSpec((LANES, width), lambda bi, t: (0, 0))

    out_spec = pl.BlockSpec((1, ts, bw), lambda bi, t: (bi, t, 0))
    out_shape = jax.ShapeDtypeStruct((b, s, bw), F32)
    return pl.pallas_call(
        _prep_body,
        grid=(b, s // ts),
        in_specs=[cur(cb, bw), cur(cb + 1, bw), cur(cb + 2, bw), cur(lb, lw),
                  prev(cb, bw), prev(cb + 1, bw), prev(cb + 2, bw), prev(lb, lw),
                  vec(bw), vec(bw), vec(bw), vec(lw), vec(bw), vec(bw), vec(bw), vec(bw), vec(bw),
                  mat(bw), mat(bw), mat(bw)],
        out_specs=[out_spec] * 8,
        out_shape=[out_shape] * 8,
        compiler_params=_params("parallel", "arbitrary"),
        name="rwkv_prep",
    )(z, z, z, z, z, z, z, z,
      p["mu_r"], p["mu_k"], p["mu_v"], p["mu_l"], p["w0"], p["a0"], p["k_k"], p["k_a"], p["r_k"],
      p["w2"], p["a2"], p["g2"])


def _bdot(a, b):
    return jnp.dot(a.astype(BF16), b.astype(BF16), preferred_element_type=F32)


def _scan_body(r_ref, k_ref, v_ref, lw_ref, kk_ref, a_ref, bonus_ref, g_ref, gng_ref, gnb_ref,
               o_ref, h_ref, *, chunks):
    c_len = RWKV_CHUNK
    hd = B_HEAD_DIM

    @pl.when(pl.program_id(2) == 0)
    def _():
        h_ref[...] = jnp.zeros_like(h_ref)

    lane = lax.broadcasted_iota(jnp.int32, (c_len, LANES), 1)
    rowc = lax.broadcasted_iota(jnp.int32, (c_len, LANES), 0)
    head0 = lane < hd
    scol = lane % hd
    strict = scol < rowc
    incl = scol <= rowc
    r2 = lax.broadcasted_iota(jnp.int32, (LANES, LANES), 0)
    c2 = lax.broadcasted_iota(jnp.int32, (LANES, LANES), 1)
    bd_mask = (r2 // hd) == (c2 // hd)
    eye = jnp.where(r2 == c2, 1.0, 0.0)
    tri = jnp.where(lax.broadcasted_iota(jnp.int32, (c_len, c_len), 1)
                    <= lax.broadcasted_iota(jnp.int32, (c_len, c_len), 0), 1.0, 0.0)
    ones_bd = _head_ones()

    def stack(x):
        return jnp.concatenate([jnp.where(head0, x, 0.0), jnp.where(head0, 0.0, x)], axis=0)

    h = h_ref[...]
    for c in range(chunks):
        rows = pl.ds(c * c_len, c_len)
        r = r_ref[0, rows, :]
        k = k_ref[0, rows, :]
        v = v_ref[0, rows, :]
        lw = lw_ref[0, rows, :]
        kk = kk_ref[0, rows, :]
        a = a_ref[0, rows, :]

        cum = jnp.dot(tri, lw, precision=lax.Precision.HIGHEST, preferred_element_type=F32)
        total = cum[c_len - 1:c_len, :]
        p_inv = jnp.exp(-cum)
        to_end = jnp.exp(total - cum)
        at = -kk * jnp.exp(cum - lw)
        rp = r * jnp.exp(cum)
        b = kk * a
        bm = b * p_inv
        km = k * p_inv
        bc_t = (b * to_end).T
        kc_t = (k * to_end).T
        pc_col = jnp.exp(cum.T[:, c_len - 1:c_len])

        lhs = jnp.concatenate([at, rp], axis=0).astype(BF16)
        rhs = jnp.concatenate([stack(bm), stack(km)], axis=0).astype(BF16)
        amat = lax.dot_general(lhs, rhs, _NT, preferred_element_type=F32)
        a_ab = jnp.where(strict, amat[0:c_len, 0:LANES], 0.0)
        a_ak = jnp.where(strict, amat[0:c_len, LANES:2 * LANES], 0.0)
        a_rb = jnp.where(incl, amat[c_len:, 0:LANES], 0.0)
        a_rk = jnp.where(incl, amat[c_len:, LANES:2 * LANES], 0.0)

        n_bd = stack(a_ab)
        t_inv = eye + n_bd
        pw = n_bd
        for _ in range(int(math.log2(c_len)) - 1):
            pw = _bdot(pw, pw)
            t_inv = t_inv + _bdot(t_inv, pw)

        v_st = stack(v)
        akv = _bdot(stack(a_ak), v_st)
        sol = _bdot(t_inv, jnp.concatenate([stack(at), akv], axis=1))
        w_st = sol[:, 0:LANES]
        u0_st = sol[:, LANES:]
        y0 = _bdot(a_rk, v_st)
        hkv = jnp.where(bd_mask, _bdot(kc_t, v), 0.0)

        hb = h.astype(BF16)
        u_st = jnp.dot(w_st.astype(BF16), hb, preferred_element_type=F32) + u0_st
        u = u_st[0:c_len] + u_st[c_len:]
        y = jnp.dot(rp.astype(BF16), hb, preferred_element_type=F32) + _bdot(a_rb, u_st) + y0
        h = pc_col * h + jnp.where(bd_mask, _bdot(bc_t, u), 0.0) + hkv

        mean = _split_dot(y, ones_bd) * (1.0 / hd)
        dlt = y - mean
        var = _split_dot(dlt * dlt, ones_bd) * (1.0 / hd)
        yn = dlt * lax.rsqrt(var + GN_EPS) * gng_ref[...] + gnb_ref[...]
        o_ref[0, rows, :] = (yn + bonus_ref[0, rows, :]) * g_ref[0, rows, :]
    h_ref[...] = h


def _rwkv_scan(prep, gn_g, gn_b, *, ts):
    b, s, bw = prep[0].shape
    assert ts % RWKV_CHUNK == 0 and s % ts == 0 and bw % LANES == 0
    tok = pl.BlockSpec((1, ts, LANES), lambda bi, p, t: (bi, t, p))
    vec = pl.BlockSpec((1, LANES), lambda bi, p, t: (0, p))
    return pl.pallas_call(
        functools.partial(_scan_body, chunks=ts // RWKV_CHUNK),
        grid=(b, bw // LANES, s // ts),
        in_specs=[tok] * 8 + [vec, vec],
        out_specs=tok,
        out_shape=jax.ShapeDtypeStruct((b, s, bw), F32),
        scratch_shapes=[pltpu.VMEM((LANES, LANES), F32)],
        compiler_params=_params("parallel", "parallel", "arbitrary"),
        name="rwkv_scan",
    )(*prep, gn_g, gn_b)


def _outproj_body(h_ref, ya_ref, yb_ref, wa_ref, wb_ref, o_ref):
    acc = jnp.dot(ya_ref[...].astype(BF16), wa_ref[...], preferred_element_type=F32)
    acc = acc + jnp.dot(yb_ref[...].astype(BF16), wb_ref[...], preferred_element_type=F32)
    o_ref[...] = h_ref[...] + acc


def _outproj(h, ya, yb, w_out, e, *, tm, tn):
    t, d = h.shape
    wa, wb = ya.shape[-1], yb.shape[-1]
    assert wa == wb
    return pl.pallas_call(
        _outproj_body,
        grid=(t // tm, d // tn),
        in_specs=[
            pl.BlockSpec((tm, tn), lambda i, j: (i, j)),
            pl.BlockSpec((tm, wa), lambda i, j: (i, 0)),
            pl.BlockSpec((tm, wb), lambda i, j: (i, 0)),
            pl.BlockSpec((None, wa, tn), lambda i, j: (e, 0, j)),
            pl.BlockSpec((None, wb, tn), lambda i, j: (e, 1, j)),
        ],
        out_specs=pl.BlockSpec((tm, tn), lambda i, j: (i, j)),
        out_shape=jax.ShapeDtypeStruct((t, d), F32),
        compiler_params=_params("parallel", "arbitrary"),
        name="outproj",
    )(h, ya, yb, w_out, w_out)


def _pool_body(h_ref, halo_ref, g_ref, w_ref, sc_ref, o_ref, *, windows, halo):
    ts, d = h_ref.shape[1], h_ref.shape[2]
    grp = d // len(windows)
    t = pl.program_id(1)
    hcur = h_ref[0]
    u = _rms(hcur, g_ref[...])
    uh = jnp.where(t == 0, 0.0, _rms(halo_ref[0], g_ref[...]))
    pos = t * ts + lax.broadcasted_iota(jnp.int32, (ts, 1), 0)
    outs = []
    for gi, win in enumerate(windows):
        cols = slice(gi * grp, (gi + 1) * grp)
        ug = u[:, cols]
        ext = jnp.concatenate([uh[:, cols], ug], axis=0)
        span = 1
        while span < win:
            ext = ext + pltpu.roll(ext, span, axis=0)
            span *= 2
        cnt = jnp.minimum(pos + 1, win).astype(F32)
        diff = ext[halo:, :] / cnt - ug
        outs.append(jnp.dot(diff.astype(BF16), w_ref[gi], preferred_element_type=F32))
    o_ref[0] = hcur + jnp.concatenate(outs, axis=-1) * sc_ref[...]


def _pool(h, gain, w_grp, o, scale, *, ts):
    b, s, d = h.shape
    halo = 2 * SUBLANES
    assert max(POOL_WINDOWS) <= halo and all(w & (w - 1) == 0 for w in POOL_WINDOWS)
    assert s % ts == 0 and ts % halo == 0
    ng, grp = w_grp.shape[1], w_grp.shape[2]
    sub = ts // halo
    return pl.pallas_call(
        functools.partial(_pool_body, windows=POOL_WINDOWS, halo=halo),
        grid=(b, s // ts),
        in_specs=[
            pl.BlockSpec((1, ts, d), lambda bi, t: (bi, t, 0)),
            pl.BlockSpec((1, halo, d), lambda bi, t: (bi, jnp.maximum(t * sub - 1, 0), 0)),
            pl.BlockSpec((1, d), lambda bi, t: (0, 0)),
            pl.BlockSpec((None, ng, grp, grp), lambda bi, t: (o, 0, 0, 0)),
            pl.BlockSpec((1, d), lambda bi, t: (0, 0)),
        ],
        out_specs=pl.BlockSpec((1, ts, d), lambda bi, t: (bi, t, 0)),
        out_shape=jax.ShapeDtypeStruct((b, s, d), F32),
        compiler_params=_params("parallel", "arbitrary"),
        name="pool",
    )(h, h, gain, w_grp, scale)


def _pick(n, pref):
    for c in range(min(pref, n), 0, -1):
        if n % c == 0 and (c % SUBLANES == 0 or c == n):
            return c
    return n


def kernel(x, ffn1_norm, ffn1_wg, ffn1_wu, ffn1_wd, mix_norm, ffn2_norm, ffn2_wg, ffn2_wu, ffn2_wd,
           ab_w_in, ab_w_out, rwkv_mu, rwkv_w0, rwkv_w2, rwkv_a0, rwkv_a2, rwkv_g2, rwkv_k_k, rwkv_k_a,
           rwkv_r_k, rwkv_gn_g, rwkv_gn_b, pool_w, pool_scale, final_norm):
    b, s, d = x.shape
    t = b * s
    depth = ffn1_norm.shape[0]
    bw = rwkv_w0.shape[-1]
    aw = ab_w_out.shape[1] - bw
    a_heads = aw // A_HEAD_DIM
    zb_col = 3 * aw
    assert rwkv_w2.shape[1] == LORA and 3 * LORA <= 2 * LANES

    w1g, w1u, w1d = ffn1_wg.astype(BF16), ffn1_wu.astype(BF16), ffn1_wd.astype(BF16)
    w2g, w2u, w2d = ffn2_wg.astype(BF16), ffn2_wu.astype(BF16), ffn2_wd.astype(BF16)
    n_in = ab_w_in.shape[-1]
    n_pad = zb_col + 3 * bw + 2 * LANES - n_in
    w_in = jnp.pad(ab_w_in, ((0, 0), (0, 0), (0, n_pad))).astype(BF16)
    w_out = ab_w_out.astype(BF16)
    pool_wb = pool_w.astype(BF16)
    zrow = jnp.zeros((LANES - LORA, bw), F32)

    tm = _pick(t, 512)
    tf = _pick(w1g.shape[-1], 512)
    row = lambda v: v.reshape(1, -1)

    h = x.reshape(t, d)
    for layer in range(depth):
        h = _ffn(h, row(ffn1_norm[layer]), w1g, w1u, w1d, layer, row(final_norm),
                 final=False, tm=tm, tf=tf)
        if layer % 2 == 0:
            e = layer // 2
            z = _inproj(h, row(mix_norm[layer]), w_in, e, tm=tm, tn=_pick(w_in.shape[-1], 1280))
            z = z.reshape(b, s, -1)
            ya = _moba(z, heads=a_heads)
            mu = rwkv_mu[e]
            prm = {
                "mu_r": row(mu[0:bw]), "mu_k": row(mu[bw:2 * bw]), "mu_v": row(mu[2 * bw:3 * bw]),
                "mu_l": row(jnp.pad(mu[3 * bw:], (0, n_pad))),
                "w0": row(rwkv_w0[e]), "a0": row(rwkv_a0[e]), "k_k": row(rwkv_k_k[e]),
                "k_a": row(rwkv_k_a[e]), "r_k": row(rwkv_r_k[e]),
                "w2": jnp.concatenate([rwkv_w2[e], zrow], axis=0).astype(BF16),
                "a2": jnp.concatenate([zrow, rwkv_a2[e]], axis=0).astype(BF16),
                "g2": jnp.concatenate([rwkv_g2[e], zrow], axis=0).astype(BF16),
            }
            prep = _rwkv_prep(z, prm, zb_col=zb_col, ts=_pick(s, 256))
            yb = _rwkv_scan(prep, row(rwkv_gn_g[e]), row(rwkv_gn_b[e]), ts=_pick(s, 256))
            h = _outproj(h, ya.reshape(t, aw), yb.reshape(t, bw), w_out, e, tm=tm, tn=_pick(d, 1024))
        else:
            o = layer // 2
            h = _pool(h.reshape(b, s, d), row(mix_norm[layer]), pool_wb, o, row(pool_scale[o]),
                      ts=_pick(s, 512)).reshape(t, d)
        h = _ffn(h, row(ffn2_norm[layer]), w2g, w2u, w2d, layer, row(final_norm),
                 final=(layer == depth - 1), tm=tm, tf=tf)
    return h.reshape(b, s, d)
```

```python
import functools
import math

import jax
import jax.numpy as jnp
from jax import lax
from jax.experimental import pallas as pl
from jax.experimental.pallas import tpu as pltpu

F32 = jnp.float32
BF16 = jnp.bfloat16

A_HEAD_DIM = 128
MOBA_BLOCK = 256
MOBA_TOPK = 3
B_HEAD_DIM = 64
LORA = 64
GN_EPS = 64e-5
RMS_EPS = 1e-6
POOL_WINDOWS = (2, 4, 8, 16)

LANES = 128
SUBLANES = 8
VMEM_LIMIT_BYTES = 56 * 1024 * 1024

NEG_BIG = -1e30
MOBA_HEAD_GROUP = 4
RWKV_CHUNK = 64

_NT = (((1,), (1,)), ((), ()))


def _rms(x, gain):
    ms = jnp.mean(x * x, axis=-1, keepdims=True)
    return x * lax.rsqrt(ms + RMS_EPS) * gain


def _params(*sem):
    return pltpu.CompilerParams(dimension_semantics=sem, vmem_limit_bytes=VMEM_LIMIT_BYTES)


def _ffn_body(h_ref, g_ref, wg_ref, wu_ref, wd_ref, fg_ref, o_ref, n_ref, acc_ref, *, final):
    f = pl.program_id(1)

    @pl.when(f == 0)
    def _():
        n_ref[...] = _rms(h_ref[...], g_ref[...]).astype(BF16)
        acc_ref[...] = jnp.zeros_like(acc_ref)

    n = n_ref[...]
    gate = jnp.dot(n, wg_ref[...], preferred_element_type=F32)
    up = jnp.dot(n, wu_ref[...], preferred_element_type=F32)
    act = (gate * jax.nn.sigmoid(gate) * up).astype(BF16)
    acc_ref[...] += jnp.dot(act, wd_ref[...], preferred_element_type=F32)

    @pl.when(f == pl.num_programs(1) - 1)
    def _():
        out = h_ref[...] + 0.5 * acc_ref[...]
        if final:
            out = _rms(out, fg_ref[...])
        o_ref[...] = out


def _ffn(h, gain, wg, wu, wd, layer, final_gain, *, final, tm, tf):
    t, d = h.shape
    f = wg.shape[-1]
    return pl.pallas_call(
        functools.partial(_ffn_body, final=final),
        grid=(t // tm, f // tf),
        in_specs=[
            pl.BlockSpec((tm, d), lambda i, j: (i, 0)),
            pl.BlockSpec((1, d), lambda i, j: (0, 0)),
            pl.BlockSpec((None, d, tf), lambda i, j: (layer, 0, j)),
            pl.BlockSpec((None, d, tf), lambda i, j: (layer, 0, j)),
            pl.BlockSpec((None, tf, d), lambda i, j: (layer, j, 0)),
            pl.BlockSpec((1, d), lambda i, j: (0, 0)),
        ],
        out_specs=pl.BlockSpec((tm, d), lambda i, j: (i, 0)),
        out_shape=jax.ShapeDtypeStruct((t, d), F32),
        scratch_shapes=[pltpu.VMEM((tm, d), BF16), pltpu.VMEM((tm, d), F32)],
        compiler_params=_params("parallel", "arbitrary"),
        name="ffn",
    )(h, gain, wg, wu, wd, final_gain)


def _inproj_body(h_ref, g_ref, w_ref, z_ref, n_ref):
    @pl.when(pl.program_id(1) == 0)
    def _():
        n_ref[...] = _rms(h_ref[...], g_ref[...]).astype(BF16)

    z_ref[...] = jnp.dot(n_ref[...], w_ref[...], preferred_element_type=F32)


def _inproj(h, gain, w_in, e, *, tm, tn):
    t, d = h.shape
    n = w_in.shape[-1]
    return pl.pallas_call(
        _inproj_body,
        grid=(t // tm, n // tn),
        in_specs=[
            pl.BlockSpec((tm, d), lambda i, j: (i, 0)),
            pl.BlockSpec((1, d), lambda i, j: (0, 0)),
            pl.BlockSpec((None, d, tn), lambda i, j: (e, 0, j)),
        ],
        out_specs=pl.BlockSpec((tm, tn), lambda i, j: (i, j)),
        out_shape=jax.ShapeDtypeStruct((t, n), F32),
        scratch_shapes=[pltpu.VMEM((tm, d), BF16)],
        compiler_params=_params("parallel", "arbitrary"),
        name="inproj",
    )(h, gain, w_in)


def _moba_body(q_ref, k_ref, v_ref, o_ref, kb_ref, vt_ref, km_ref, sel_ref, *, hg, nb, blk, topk, scale):
    i = pl.program_id(2)
    dh = A_HEAD_DIM
    cols = [slice(hh * dh, (hh + 1) * dh) for hh in range(hg)]

    @pl.when(i == 0)
    def _():
        km_ref[...] = jnp.zeros_like(km_ref)
        for hh in range(hg):
            for jb in range(nb):
                rows = slice(jb * blk, (jb + 1) * blk)
                kf = k_ref[0, rows, cols[hh]]
                kb_ref[hh, jb] = kf.astype(BF16)
                vt_ref[hh, jb] = v_ref[0, rows, cols[hh]].T.astype(BF16)
                km_ref[hh, jb:jb + 1, :] = jnp.mean(kf, axis=0, keepdims=True)

    qs = [q_ref[0, :, c] for c in cols]
    for hh in range(hg):
        gate = lax.dot_general(km_ref[hh], qs[hh], _NT, precision=lax.Precision.HIGHEST,
                               preferred_element_type=F32)
        blk_id = lax.broadcasted_iota(jnp.int32, gate.shape, 0)
        rank = jnp.zeros(gate.shape, jnp.int32)
        for jp in range(nb):
            row = gate[jp:jp + 1, :]
            beats = (row > gate) | ((row == gate) & (jp < blk_id))
            rank = rank + jnp.where(beats & (jp < i), 1, 0)
        sel_ref[hh] = jnp.where((blk_id < i) & (rank < topk), 0.0, NEG_BIG)

    qb = [q.astype(BF16) for q in qs]

    def scores(hh, j):
        return lax.dot_general(kb_ref[hh, j], qb[hh], _NT, preferred_element_type=F32) * scale

    def pv(hh, p, j):
        return jnp.dot(vt_ref[hh, j], p.astype(BF16), preferred_element_type=F32)

    kpos = lax.broadcasted_iota(jnp.int32, (blk, blk), 0)
    qpos = lax.broadcasted_iota(jnp.int32, (blk, blk), 1)
    causal = kpos <= qpos
    heads = range(hg)
    ss = [jnp.where(causal, scores(hh, i), NEG_BIG) for hh in heads]
    ms = [jnp.max(s, axis=0, keepdims=True) for s in ss]
    ps = [jnp.exp(s - m) for s, m in zip(ss, ms)]
    ls = [jnp.sum(p, axis=0, keepdims=True) for p in ps]
    accs = [pv(hh, ps[hh], i) for hh in heads]

    def past(j, carry):
        ms, ls, accs = carry
        ss = [scores(hh, j) + sel_ref[hh, pl.ds(j, 1), :] for hh in heads]
        m_new = [jnp.maximum(m, jnp.max(s, axis=0, keepdims=True)) for m, s in zip(ms, ss)]
        alpha = [jnp.exp(m - mn) for m, mn in zip(ms, m_new)]
        ps = [jnp.exp(s - mn) for s, mn in zip(ss, m_new)]
        ls = [a * l + jnp.sum(p, axis=0, keepdims=True) for a, l, p in zip(alpha, ls, ps)]
        pvs = [pv(hh, ps[hh], j) for hh in heads]
        accs = [a * acc + x for a, acc, x in zip(alpha, accs, pvs)]
        return tuple(m_new), tuple(ls), tuple(accs)

    _, ls, accs = lax.fori_loop(0, i, past, (tuple(ms), tuple(ls), tuple(accs)))
    final = [(None, ls[hh], accs[hh]) for hh in heads]
    for hh in range(hg):
        _, l, acc = final[hh]
        o_ref[0, :, cols[hh]] = (acc / l).T


def _moba(z, *, heads, hg):
    b, s, _ = z.shape
    dh, blk = A_HEAD_DIM, MOBA_BLOCK
    nb = s // blk
    nbp = -(-nb // SUBLANES) * SUBLANES
    ng = heads // hg
    assert s % blk == 0 and heads % hg == 0
    return pl.pallas_call(
        functools.partial(_moba_body, hg=hg, nb=nb, blk=blk, topk=min(MOBA_TOPK, nb), scale=dh ** -0.5),
        grid=(b, ng, nb),
        in_specs=[
            pl.BlockSpec((1, blk, hg * dh), lambda bi, g, i: (bi, i, g)),
            pl.BlockSpec((1, s, hg * dh), lambda bi, g, i: (bi, 0, ng + g)),
            pl.BlockSpec((1, s, hg * dh), lambda bi, g, i: (bi, 0, 2 * ng + g)),
        ],
        out_specs=pl.BlockSpec((1, blk, hg * dh), lambda bi, g, i: (bi, i, g)),
        out_shape=jax.ShapeDtypeStruct((b, s, heads * dh), F32),
        scratch_shapes=[pltpu.VMEM((hg, nb, blk, dh), BF16), pltpu.VMEM((hg, nb, dh, blk), BF16),
                        pltpu.VMEM((hg, nbp, dh), F32), pltpu.VMEM((hg, nbp, blk), F32)],
        compiler_params=_params("parallel", "parallel", "arbitrary"),
        name="moba",
    )(z, z, z)


def _split_dot(x, ones_bd):
    hi = x.astype(BF16)
    lo = (x - hi.astype(F32)).astype(BF16)
    return (jnp.dot(hi, ones_bd, preferred_element_type=F32)
            + jnp.dot(lo, ones_bd, preferred_element_type=F32))


def _head_sum(x, ones_bd):
    cols = [_split_dot(x[:, c:c + LANES], ones_bd) for c in range(0, x.shape[-1], LANES)]
    return cols[0] if len(cols) == 1 else jnp.concatenate(cols, axis=-1)


def _head_ones():
    r = lax.broadcasted_iota(jnp.int32, (LANES, LANES), 0) // B_HEAD_DIM
    c = lax.broadcasted_iota(jnp.int32, (LANES, LANES), 1) // B_HEAD_DIM
    return jnp.where(r == c, 1.0, 0.0).astype(BF16)


def _prep_body(zr_ref, zk_ref, zv_ref, zl_ref, pr_ref, pk_ref, pv_ref, plo_ref,
               mur_ref, muk_ref, muv_ref, mul_ref, w0_ref, a0_ref, kk_ref, ka_ref, rk_ref,
               w2_ref, a2_ref, g2_ref,
               r_out, k_out, v_out, lw_out, kk_out, a_out, bonus_out, g_out):
    ts = zr_ref.shape[1]
    first = pl.program_id(1) == 0
    row = lax.broadcasted_iota(jnp.int32, (ts, 1), 0)

    def lerp(z_ref, prev_ref, mu_ref):
        x = z_ref[0]
        prev = jnp.where(first, 0.0, prev_ref[0, SUBLANES - 1:SUBLANES, :])
        shifted = jnp.where(row == 0, prev, pltpu.roll(x, 1, axis=0))
        return x + (shifted - x) * mu_ref[...]

    r = lerp(zr_ref, pr_ref, mur_ref)
    k = lerp(zk_ref, pk_ref, muk_ref)
    v = lerp(zv_ref, pv_ref, muv_ref)
    lo = lerp(zl_ref, plo_ref, mul_ref)
    wa = lo[:, 0:LANES]
    gl = lo[:, LANES:2 * LANES]

    w_in = w0_ref[...] + jnp.dot(jnp.tanh(wa).astype(BF16), w2_ref[...], preferred_element_type=F32)
    sp = jnp.maximum(-w_in, 0.0) + jnp.log(1.0 + jnp.exp(-jnp.abs(w_in)))
    w = -sp - 0.5
    lw = -jnp.exp(w)
    a = jax.nn.sigmoid(a0_ref[...] + jnp.dot(wa.astype(BF16), a2_ref[...], preferred_element_type=F32))
    g = jnp.dot(jax.nn.sigmoid(gl).astype(BF16), g2_ref[...], preferred_element_type=F32)

    ones_bd = _head_ones()
    kk = k * kk_ref[...]
    kk = kk * lax.rsqrt(jnp.maximum(_head_sum(kk * kk, ones_bd), 1e-24))
    k2 = k * (1.0 + (a - 1.0) * ka_ref[...])
    bonus = _head_sum(r * k2 * rk_ref[...], ones_bd) * v

    r_out[0] = r
    k_out[0] = k2
    v_out[0] = v
    lw_out[0] = lw
    kk_out[0] = kk
    a_out[0] = a
    bonus_out[0] = bonus
    g_out[0] = g


def _rwkv_prep(z, p, *, zb_col, ts):
    b, s, _ = z.shape
    bw = p["w0"].shape[-1]
    lw = 2 * LANES
    cb = zb_col // bw
    lb = (zb_col + 3 * bw) // lw
    assert zb_col % bw == 0 and (zb_col + 3 * bw) % lw == 0 and s % ts == 0 and ts % SUBLANES == 0
    sub = ts // SUBLANES

    def cur(col, width):
        return pl.BlockSpec((1, ts, width), lambda bi, t: (bi, t, col))

    def prev(col, width):
        return pl.BlockSpec((1, SUBLANES, width), lambda bi, t: (bi, jnp.maximum(t * sub - 1, 0), col))

    def vec(width):
        return pl.BlockSpec((1, width), lambda bi, t: (0, 0))

    def mat(width):
        return pl.BlockSpec((LANES, width), lambda bi, t: (0, 0))

    out_spec = pl.BlockSpec((1, ts, bw), lambda bi, t: (bi, t, 0))
    out_shape = jax.ShapeDtypeStruct((b, s, bw), F32)
    return pl.pallas_call(
        _prep_body,
        grid=(b, s // ts),
        in_specs=[cur(cb, bw), cur(cb + 1, bw), cur(cb + 2, bw), cur(lb, lw),
                  prev(cb, bw), prev(cb + 1, bw), prev(cb + 2, bw), prev(lb, lw),
                  vec(bw), vec(bw), vec(bw), vec(lw), vec(bw), vec(bw), vec(bw), vec(bw), vec(bw),
                  mat(bw), mat(bw), mat(bw)],
        out_specs=[out_spec] * 8,
        out_shape=[out_shape] * 8,
        compiler_params=_params("parallel", "arbitrary"),
        name="rwkv_prep",
    )(z, z, z, z, z, z, z, z,
      p["mu_r"], p["mu_k"], p["mu_v"], p["mu_l"], p["w0"], p["a0"], p["k_k"], p["k_a"], p["r_k"],
      p["w2"], p["a2"], p["g2"])


def _bdot(a, b):
    return jnp.dot(a.astype(BF16), b.astype(BF16), preferred_element_type=F32)


def _scan_body(r_ref, k_ref, v_ref, lw_ref, kk_ref, a_ref, bonus_ref, g_ref, gng_ref, gnb_ref,
               o_ref, h_ref, *, chunks, pairs):
    c_len = RWKV_CHUNK
    hd = B_HEAD_DIM
    prs = range(pairs)
    lanes = [slice(p * LANES, (p + 1) * LANES) for p in prs]

    @pl.when(pl.program_id(1) == 0)
    def _():
        h_ref[...] = jnp.zeros_like(h_ref)

    lane = lax.broadcasted_iota(jnp.int32, (c_len, LANES), 1)
    rowc = lax.broadcasted_iota(jnp.int32, (c_len, LANES), 0)
    head0 = lane < hd
    scol = lane % hd
    strict = scol < rowc
    incl = scol <= rowc
    r2 = lax.broadcasted_iota(jnp.int32, (LANES, LANES), 0)
    c2 = lax.broadcasted_iota(jnp.int32, (LANES, LANES), 1)
    bd_mask = (r2 // hd) == (c2 // hd)
    eye = jnp.where(r2 == c2, 1.0, 0.0)
    tri = jnp.where(lax.broadcasted_iota(jnp.int32, (c_len, c_len), 1)
                    <= lax.broadcasted_iota(jnp.int32, (c_len, c_len), 0), 1.0, 0.0)
    ones_bd = _head_ones()
    gng = [gng_ref[:, ln] for ln in lanes]
    gnb = [gnb_ref[:, ln] for ln in lanes]

    def stack(x):
        return jnp.concatenate([jnp.where(head0, x, 0.0), jnp.where(head0, 0.0, x)], axis=0)

    def chunk_step(c, carry):
        rows = pl.ds(pl.multiple_of(c * c_len, c_len), c_len)

        def load(ref):
            return [ref[0, rows, ln] for ln in lanes]

        r, k, v, lw, kk, a = load(r_ref), load(k_ref), load(v_ref), load(lw_ref), load(kk_ref), load(a_ref)
        cum = [jnp.dot(tri, x, precision=lax.Precision.HIGHEST, preferred_element_type=F32) for x in lw]
        p_inv = [jnp.exp(-x) for x in cum]
        to_end = [jnp.exp(x[c_len - 1:c_len, :] - x) for x in cum]
        at = [-kk[p] * jnp.exp(cum[p] - lw[p]) for p in prs]
        rp = [r[p] * jnp.exp(cum[p]) for p in prs]
        b = [kk[p] * a[p] for p in prs]
        bm = [b[p] * p_inv[p] for p in prs]
        km = [k[p] * p_inv[p] for p in prs]
        bc_t = [(b[p] * to_end[p]).T for p in prs]
        kc_t = [(k[p] * to_end[p]).T for p in prs]
        pc_col = [jnp.exp(x.T[:, c_len - 1:c_len]) for x in cum]

        lhs = [jnp.concatenate([at[p], rp[p]], axis=0).astype(BF16) for p in prs]
        rhs = [jnp.concatenate([stack(bm[p]), stack(km[p])], axis=0).astype(BF16) for p in prs]
        amat = [lax.dot_general(lhs[p], rhs[p], _NT, preferred_element_type=F32) for p in prs]
        a_ab = [jnp.where(strict, m[0:c_len, 0:LANES], 0.0) for m in amat]
        a_ak = [jnp.where(strict, m[0:c_len, LANES:2 * LANES], 0.0) for m in amat]
        a_rb = [jnp.where(incl, m[c_len:, 0:LANES], 0.0) for m in amat]
        a_rk = [jnp.where(incl, m[c_len:, LANES:2 * LANES], 0.0) for m in amat]

        pw = [stack(x) for x in a_ab]
        t_inv = [eye + x for x in pw]
        for _ in range(int(math.log2(c_len)) - 1):
            pw = [_bdot(x, x) for x in pw]
            t_inv = [t_inv[p] + _bdot(t_inv[p], pw[p]) for p in prs]

        v_st = [stack(x) for x in v]
        akv = [_bdot(stack(a_ak[p]), v_st[p]) for p in prs]
        sol = [_bdot(t_inv[p], jnp.concatenate([stack(at[p]), akv[p]], axis=1)) for p in prs]
        y0 = [_bdot(a_rk[p], v_st[p]) for p in prs]
        hkv = [jnp.where(bd_mask, _bdot(kc_t[p], v[p]), 0.0) for p in prs]

        h = [h_ref[p] for p in prs]
        hb = [x.astype(BF16) for x in h]
        u_st = [jnp.dot(sol[p][:, 0:LANES].astype(BF16), hb[p], preferred_element_type=F32)
                + sol[p][:, LANES:] for p in prs]
        y = [jnp.dot(rp[p].astype(BF16), hb[p], preferred_element_type=F32) + y0[p] for p in prs]
        y = [y[p] + _bdot(a_rb[p], u_st[p]) for p in prs]
        u = [x[0:c_len] + x[c_len:] for x in u_st]
        for p in prs:
            h_ref[p] = pc_col[p] * h[p] + jnp.where(bd_mask, _bdot(bc_t[p], u[p]), 0.0) + hkv[p]

        mean = [_split_dot(x, ones_bd) * (1.0 / hd) for x in y]
        dlt = [y[p] - mean[p] for p in prs]
        var = [_split_dot(x * x, ones_bd) * (1.0 / hd) for x in dlt]
        for p in prs:
            yn = dlt[p] * lax.rsqrt(var[p] + GN_EPS) * gng[p] + gnb[p]
            o_ref[0, rows, lanes[p]] = (yn + bonus_ref[0, rows, lanes[p]]) * g_ref[0, rows, lanes[p]]
        return carry

    lax.fori_loop(0, chunks, chunk_step, 0)


def _rwkv_scan(prep, gn_g, gn_b, *, ts):
    b, s, bw = prep[0].shape
    assert ts % RWKV_CHUNK == 0 and s % ts == 0 and bw % LANES == 0
    pairs = bw // LANES
    tok = pl.BlockSpec((1, ts, bw), lambda bi, t: (bi, t, 0))
    vec = pl.BlockSpec((1, bw), lambda bi, t: (0, 0))
    return pl.pallas_call(
        functools.partial(_scan_body, chunks=ts // RWKV_CHUNK, pairs=pairs),
        grid=(b, s // ts),
        in_specs=[tok] * 8 + [vec, vec],
        out_specs=tok,
        out_shape=jax.ShapeDtypeStruct((b, s, bw), F32),
        scratch_shapes=[pltpu.VMEM((pairs, LANES, LANES), F32)],
        compiler_params=_params("parallel", "arbitrary"),
        name="rwkv_scan",
    )(*prep, gn_g, gn_b)


def _outproj_body(h_ref, ya_ref, yb_ref, wa_ref, wb_ref, o_ref):
    acc = jnp.dot(ya_ref[...].astype(BF16), wa_ref[...], preferred_element_type=F32)
    acc = acc + jnp.dot(yb_ref[...].astype(BF16), wb_ref[...], preferred_element_type=F32)
    o_ref[...] = h_ref[...] + acc


def _outproj(h, ya, yb, w_out, e, *, tm, tn):
    t, d = h.shape
    wa, wb = ya.shape[-1], yb.shape[-1]
    assert wa == wb
    return pl.pallas_call(
        _outproj_body,
        grid=(t // tm, d // tn),
        in_specs=[
            pl.BlockSpec((tm, tn), lambda i, j: (i, j)),
            pl.BlockSpec((tm, wa), lambda i, j: (i, 0)),
            pl.BlockSpec((tm, wb), lambda i, j: (i, 0)),
            pl.BlockSpec((None, wa, tn), lambda i, j: (e, 0, j)),
            pl.BlockSpec((None, wb, tn), lambda i, j: (e, 1, j)),
        ],
        out_specs=pl.BlockSpec((tm, tn), lambda i, j: (i, j)),
        out_shape=jax.ShapeDtypeStruct((t, d), F32),
        compiler_params=_params("parallel", "arbitrary"),
        name="outproj",
    )(h, ya, yb, w_out, w_out)


def _pool_body(h_ref, halo_ref, g_ref, w_ref, sc_ref, o_ref, *, windows, halo):
    ts, d = h_ref.shape[1], h_ref.shape[2]
    grp = d // len(windows)
    t = pl.program_id(1)
    hcur = h_ref[0]
    u = _rms(hcur, g_ref[...])
    uh = jnp.where(t == 0, 0.0, _rms(halo_ref[0], g_ref[...]))
    pos = t * ts + lax.broadcasted_iota(jnp.int32, (ts, 1), 0)
    outs = []
    for gi, win in enumerate(windows):
        cols = slice(gi * grp, (gi + 1) * grp)
        ug = u[:, cols]
        ext = jnp.concatenate([uh[:, cols], ug], axis=0)
        span = 1
        while span < win:
            ext = ext + pltpu.roll(ext, span, axis=0)
            span *= 2
        cnt = jnp.minimum(pos + 1, win).astype(F32)
        diff = ext[halo:, :] / cnt - ug
        outs.append(jnp.dot(diff.astype(BF16), w_ref[gi], preferred_element_type=F32))
    o_ref[0] = hcur + jnp.concatenate(outs, axis=-1) * sc_ref[...]


def _pool(h, gain, w_grp, o, scale, *, ts):
    b, s, d = h.shape
    halo = 2 * SUBLANES
    assert max(POOL_WINDOWS) <= halo and all(w & (w - 1) == 0 for w in POOL_WINDOWS)
    assert s % ts == 0 and ts % halo == 0
    ng, grp = w_grp.shape[1], w_grp.shape[2]
    sub = ts // halo
    return pl.pallas_call(
        functools.partial(_pool_body, windows=POOL_WINDOWS, halo=halo),
        grid=(b, s // ts),
        in_specs=[
            pl.BlockSpec((1, ts, d), lambda bi, t: (bi, t, 0)),
            pl.BlockSpec((1, halo, d), lambda bi, t: (bi, jnp.maximum(t * sub - 1, 0), 0)),
            pl.BlockSpec((1, d), lambda bi, t: (0, 0)),
            pl.BlockSpec((None, ng, grp, grp), lambda bi, t: (o, 0, 0, 0)),
            pl.BlockSpec((1, d), lambda bi, t: (0, 0)),
        ],
        out_specs=pl.BlockSpec((1, ts, d), lambda bi, t: (bi, t, 0)),
        out_shape=jax.ShapeDtypeStruct((b, s, d), F32),
        compiler_params=_params("parallel", "arbitrary"),
        name="pool",
    )(h, h, gain, w_grp, scale)


def _pick(n, pref):
    for c in range(min(pref, n), 0, -1):
        if n % c == 0 and (c % SUBLANES == 0 or c == n):
            return c
    return n


def kernel(x, ffn1_norm, ffn1_wg, ffn1_wu, ffn1_wd, mix_norm, ffn2_norm, ffn2_wg, ffn2_wu, ffn2_wd,
           ab_w_in, ab_w_out, rwkv_mu, rwkv_w0, rwkv_w2, rwkv_a0, rwkv_a2, rwkv_g2, rwkv_k_k, rwkv_k_a,
           rwkv_r_k, rwkv_gn_g, rwkv_gn_b, pool_w, pool_scale, final_norm):
    b, s, d = x.shape
    t = b * s
    depth = ffn1_norm.shape[0]
    bw = rwkv_w0.shape[-1]
    aw = ab_w_out.shape[1] - bw
    a_heads = aw // A_HEAD_DIM
    zb_col = 3 * aw
    assert rwkv_w2.shape[1] == LORA and 3 * LORA <= 2 * LANES

    w1g, w1u, w1d = ffn1_wg.astype(BF16), ffn1_wu.astype(BF16), ffn1_wd.astype(BF16)
    w2g, w2u, w2d = ffn2_wg.astype(BF16), ffn2_wu.astype(BF16), ffn2_wd.astype(BF16)
    n_in = ab_w_in.shape[-1]
    n_pad = zb_col + 3 * bw + 2 * LANES - n_in
    w_in = jnp.pad(ab_w_in, ((0, 0), (0, 0), (0, n_pad))).astype(BF16)
    w_out = ab_w_out.astype(BF16)
    pool_wb = pool_w.astype(BF16)
    zrow = jnp.zeros((LANES - LORA, bw), F32)

    tm = _pick(t, 512)
    tf = _pick(w1g.shape[-1], 512)
    row = lambda v: v.reshape(1, -1)

    h = x.reshape(t, d)
    for layer in range(depth):
        h = _ffn(h, row(ffn1_norm[layer]), w1g, w1u, w1d, layer, row(final_norm),
                 final=False, tm=tm, tf=tf)
        if layer % 2 == 0:
            e = layer // 2
            z = _inproj(h, row(mix_norm[layer]), w_in, e, tm=tm, tn=_pick(w_in.shape[-1], 1280))
            z = z.reshape(b, s, -1)
            ya = _moba(z, heads=a_heads, hg=math.gcd(a_heads, MOBA_HEAD_GROUP))
            mu = rwkv_mu[e]
            prm = {
                "mu_r": row(mu[0:bw]), "mu_k": row(mu[bw:2 * bw]), "mu_v": row(mu[2 * bw:3 * bw]),
                "mu_l": row(jnp.pad(mu[3 * bw:], (0, n_pad))),
                "w0": row(rwkv_w0[e]), "a0": row(rwkv_a0[e]), "k_k": row(rwkv_k_k[e]),
                "k_a": row(rwkv_k_a[e]), "r_k": row(rwkv_r_k[e]),
                "w2": jnp.concatenate([rwkv_w2[e], zrow], axis=0).astype(BF16),
                "a2": jnp.concatenate([zrow, rwkv_a2[e]], axis=0).astype(BF16),
                "g2": jnp.concatenate([rwkv_g2[e], zrow], axis=0).astype(BF16),
            }
            prep = _rwkv_prep(z, prm, zb_col=zb_col, ts=_pick(s, 256))
            yb = _rwkv_scan(prep, row(rwkv_gn_g[e]), row(rwkv_gn_b[e]), ts=_pick(s, 256))
            h = _outproj(h, ya.reshape(t, aw), yb.reshape(t, bw), w_out, e, tm=tm, tn=_pick(d, 1024))
        else:
            o = layer // 2
            h = _pool(h.reshape(b, s, d), row(mix_norm[layer]), pool_wb, o, row(pool_scale[o]),
                      ts=_pick(s, 512)).reshape(t, d)
        h = _ffn(h, row(ffn2_norm[layer]), w2g, w2u, w2d, layer, row(final_norm),
                 final=(layer == depth - 1), tm=tm, tf=tf)
    return h.reshape(b, s, d)
```

```python
import functools
import math

import jax
import jax.numpy as jnp
from jax import lax
from jax.experimental import pallas as pl
from jax.experimental.pallas import tpu as pltpu

F32 = jnp.float32
BF16 = jnp.bfloat16

A_HEAD_DIM = 128
MOBA_BLOCK = 256
MOBA_TOPK = 3
B_HEAD_DIM = 64
LORA = 64
GN_EPS = 64e-5
RMS_EPS = 1e-6
POOL_WINDOWS = (2, 4, 8, 16)

LANES = 128
SUBLANES = 8
VMEM_LIMIT_BYTES = 56 * 1024 * 1024

NEG_BIG = -1e30
MOBA_HEAD_GROUP = 4
RWKV_CHUNK = 64

_NT = (((1,), (1,)), ((), ()))


def _rms(x, gain):
    ms = jnp.mean(x * x, axis=-1, keepdims=True)
    return x * lax.rsqrt(ms + RMS_EPS) * gain


def _params(*sem):
    return pltpu.CompilerParams(dimension_semantics=sem, vmem_limit_bytes=VMEM_LIMIT_BYTES)


def _ffn_body(h_ref, g_ref, wg_ref, wu_ref, wd_ref, fg_ref, o_ref, n_ref, acc_ref, *, final):
    f = pl.program_id(1)

    @pl.when(f == 0)
    def _():
        n_ref[...] = _rms(h_ref[...], g_ref[...]).astype(BF16)
        acc_ref[...] = jnp.zeros_like(acc_ref)

    n = n_ref[...]
    gate = jnp.dot(n, wg_ref[...], preferred_element_type=F32)
    up = jnp.dot(n, wu_ref[...], preferred_element_type=F32)
    act = (gate * jax.nn.sigmoid(gate) * up).astype(BF16)
    acc_ref[...] += jnp.dot(act, wd_ref[...], preferred_element_type=F32)

    @pl.when(f == pl.num_programs(1) - 1)
    def _():
        out = h_ref[...] + 0.5 * acc_ref[...]
        if final:
            out = _rms(out, fg_ref[...])
        o_ref[...] = out


def _ffn(h, gain, wg, wu, wd, layer, final_gain, *, final, tm, tf):
    t, d = h.shape
    f = wg.shape[-1]
    return pl.pallas_call(
        functools.partial(_ffn_body, final=final),
        grid=(t // tm, f // tf),
        in_specs=[
            pl.BlockSpec((tm, d), lambda i, j: (i, 0), pipeline_mode=pl.Buffered(1)),
            pl.BlockSpec((1, d), lambda i, j: (0, 0)),
            pl.BlockSpec((None, d, tf), lambda i, j: (layer, 0, j)),
            pl.BlockSpec((None, d, tf), lambda i, j: (layer, 0, j)),
            pl.BlockSpec((None, tf, d), lambda i, j: (layer, j, 0)),
            pl.BlockSpec((1, d), lambda i, j: (0, 0)),
        ],
        out_specs=pl.BlockSpec((tm, d), lambda i, j: (i, 0), pipeline_mode=pl.Buffered(1)),
        out_shape=jax.ShapeDtypeStruct((t, d), F32),
        scratch_shapes=[pltpu.VMEM((tm, d), BF16), pltpu.VMEM((tm, d), F32)],
        compiler_params=_params("parallel", "arbitrary"),
        name="ffn",
    )(h, gain, wg, wu, wd, final_gain)


def _inproj_body(h_ref, g_ref, w_ref, z_ref, n_ref):
    @pl.when(pl.program_id(1) == 0)
    def _():
        n_ref[...] = _rms(h_ref[...], g_ref[...]).astype(BF16)

    z_ref[...] = jnp.dot(n_ref[...], w_ref[...], preferred_element_type=F32)


def _inproj(h, gain, w_in, e, *, tm, tn):
    t, d = h.shape
    n = w_in.shape[-1]
    return pl.pallas_call(
        _inproj_body,
        grid=(t // tm, n // tn),
        in_specs=[
            pl.BlockSpec((tm, d), lambda i, j: (i, 0), pipeline_mode=pl.Buffered(1)),
            pl.BlockSpec((1, d), lambda i, j: (0, 0)),
            pl.BlockSpec((None, d, tn), lambda i, j: (e, 0, j)),
        ],
        out_specs=pl.BlockSpec((tm, tn), lambda i, j: (i, j)),
        out_shape=jax.ShapeDtypeStruct((t, n), F32),
        scratch_shapes=[pltpu.VMEM((tm, d), BF16)],
        compiler_params=_params("parallel", "arbitrary"),
        name="inproj",
    )(h, gain, w_in)


def _moba_body(q_ref, k_ref, v_ref, o_ref, kb_ref, vt_ref, km_ref, sel_ref, s0_ref, s1_ref, acc_ref,
               *, hg, nb, blk, topk, scale):
    i = pl.program_id(2)
    dh = A_HEAD_DIM
    cols = [slice(hh * dh, (hh + 1) * dh) for hh in range(hg)]

    @pl.when(i == 0)
    def _():
        km_ref[...] = jnp.zeros_like(km_ref)
        for hh in range(hg):
            for jb in range(nb):
                rows = slice(jb * blk, (jb + 1) * blk)
                kf = k_ref[0, rows, cols[hh]]
                kb_ref[hh, jb] = kf.astype(BF16)
                vt_ref[hh, jb] = v_ref[0, rows, cols[hh]].T.astype(BF16)
                km_ref[hh, jb:jb + 1, :] = jnp.mean(kf, axis=0, keepdims=True)

    qs = [q_ref[0, :, c] for c in cols]
    for hh in range(hg):
        gate = lax.dot_general(km_ref[hh], qs[hh], _NT, precision=lax.Precision.HIGHEST,
                               preferred_element_type=F32)
        blk_id = lax.broadcasted_iota(jnp.int32, gate.shape, 0)
        rank = jnp.zeros(gate.shape, jnp.int32)
        for jp in range(nb):
            row = gate[jp:jp + 1, :]
            beats = (row > gate) | ((row == gate) & (jp < blk_id))
            rank = rank + jnp.where(beats & (jp < i), 1, 0)
        sel_ref[hh] = jnp.where((blk_id < i) & (rank < topk), 0.0, NEG_BIG)

    qb = [(q * scale).astype(BF16) for q in qs]

    def scores(hh, j):
        return lax.dot_general(kb_ref[hh, j], qb[hh], _NT, preferred_element_type=F32)

    def pv(hh, p, j):
        return jnp.dot(vt_ref[hh, j], p.astype(BF16), preferred_element_type=F32)

    kpos = lax.broadcasted_iota(jnp.int32, (blk, blk), 0)
    qpos = lax.broadcasted_iota(jnp.int32, (blk, blk), 1)
    causal = kpos <= qpos
    heads = range(hg)
    ss = [jnp.where(causal, scores(hh, i), NEG_BIG) for hh in heads]
    ms = [jnp.max(s, axis=0, keepdims=True) for s in ss]
    ps = [jnp.exp(s - m) for s, m in zip(ss, ms)]
    ls = [jnp.sum(p, axis=0, keepdims=True) for p in ps]
    accs = [pv(hh, ps[hh], i) for hh in heads]

    for hh in heads:
        acc_ref[hh] = accs[hh]

    def produce(j, s_ref):
        mx = []
        for hh in heads:
            s = scores(hh, jnp.minimum(j, nb - 1))
            s_ref[hh] = s
            mx.append(jnp.max(s, axis=0, keepdims=True))
        return tuple(mx)

    def consume(j, s_ref, ms, ls, mx):
        sel = [sel_ref[hh, pl.ds(j, 1), :] for hh in heads]
        m_new = [jnp.maximum(m, x + b) for m, x, b in zip(ms, mx, sel)]
        alpha = [jnp.exp(m - mn) for m, mn in zip(ms, m_new)]
        shift = [mn - 2.0 * b for mn, b in zip(m_new, sel)]
        ps = [jnp.exp(s_ref[hh] - shift[hh]) for hh in heads]
        ls = [a * l + jnp.sum(p, axis=0, keepdims=True) for a, l, p in zip(alpha, ls, ps)]
        pvs = [pv(hh, ps[hh], j) for hh in heads]
        for hh in heads:
            acc_ref[hh] = alpha[hh] * acc_ref[hh] + pvs[hh]
        return tuple(m_new), tuple(ls)

    def past(jj, carry):
        ms, ls, mx0 = carry
        j0 = 2 * jj
        mx1 = produce(j0 + 1, s1_ref)
        ms, ls = consume(j0, s0_ref, ms, ls, mx0)
        mx0 = produce(j0 + 2, s0_ref)
        ms, ls = consume(j0 + 1, s1_ref, ms, ls, mx1)
        return ms, ls, mx0

    mx0 = produce(0, s0_ref)
    _, ls, _ = lax.fori_loop(0, (i + 1) // 2, past, (tuple(ms), tuple(ls), mx0))
    for hh in heads:
        o_ref[0, :, cols[hh]] = (acc_ref[hh] / ls[hh]).T


def _moba(z, *, heads, hg):
    b, s, _ = z.shape
    dh, blk = A_HEAD_DIM, MOBA_BLOCK
    nb = s // blk
    nbp = -(-nb // SUBLANES) * SUBLANES
    ng = heads // hg
    assert s % blk == 0 and heads % hg == 0
    return pl.pallas_call(
        functools.partial(_moba_body, hg=hg, nb=nb, blk=blk, topk=min(MOBA_TOPK, nb), scale=dh ** -0.5),
        grid=(b, ng, nb),
        in_specs=[
            pl.BlockSpec((1, blk, hg * dh), lambda bi, g, i: (bi, i, g)),
            pl.BlockSpec((1, s, hg * dh), lambda bi, g, i: (bi, 0, ng + g)),
            pl.BlockSpec((1, s, hg * dh), lambda bi, g, i: (bi, 0, 2 * ng + g)),
        ],
        out_specs=pl.BlockSpec((1, blk, hg * dh), lambda bi, g, i: (bi, i, g)),
        out_shape=jax.ShapeDtypeStruct((b, s, heads * dh), F32),
        scratch_shapes=[pltpu.VMEM((hg, nb, blk, dh), BF16), pltpu.VMEM((hg, nb, dh, blk), BF16),
                        pltpu.VMEM((hg, nbp, dh), F32), pltpu.VMEM((hg, nbp, blk), F32),
                        pltpu.VMEM((hg, blk, blk), F32), pltpu.VMEM((hg, blk, blk), F32),
                        pltpu.VMEM((hg, dh, blk), F32)],
        compiler_params=_params("parallel", "parallel", "arbitrary"),
        name="moba",
    )(z, z, z)


def _split_dot(x, ones_bd):
    hi = x.astype(BF16)
    lo = (x - hi.astype(F32)).astype(BF16)
    return (jnp.dot(hi, ones_bd, preferred_element_type=F32)
            + jnp.dot(lo, ones_bd, preferred_element_type=F32))


def _head_sum(x, ones_bd):
    cols = [_split_dot(x[:, c:c + LANES], ones_bd) for c in range(0, x.shape[-1], LANES)]
    return cols[0] if len(cols) == 1 else jnp.concatenate(cols, axis=-1)


def _head_ones():
    r = lax.broadcasted_iota(jnp.int32, (LANES, LANES), 0) // B_HEAD_DIM
    c = lax.broadcasted_iota(jnp.int32, (LANES, LANES), 1) // B_HEAD_DIM
    return jnp.where(r == c, 1.0, 0.0).astype(BF16)


def _prep_body(zr_ref, zk_ref, zv_ref, zl_ref, pr_ref, pk_ref, pv_ref, plo_ref,
               mur_ref, muk_ref, muv_ref, mul_ref, w0_ref, a0_ref, kk_ref, ka_ref, rk_ref,
               w2_ref, a2_ref, g2_ref,
               r_out, k_out, v_out, lw_out, kk_out, a_out, bonus_out, g_out):
    ts = zr_ref.shape[1]
    first = pl.program_id(1) == 0
    row = lax.broadcasted_iota(jnp.int32, (ts, 1), 0)

    def lerp(z_ref, prev_ref, mu_ref):
        x = z_ref[0]
        prev = jnp.where(first, 0.0, prev_ref[0, SUBLANES - 1:SUBLANES, :])
        shifted = jnp.where(row == 0, prev, pltpu.roll(x, 1, axis=0))
        return x + (shifted - x) * mu_ref[...]

    r = lerp(zr_ref, pr_ref, mur_ref)
    k = lerp(zk_ref, pk_ref, muk_ref)
    v = lerp(zv_ref, pv_ref, muv_ref)
    lo = lerp(zl_ref, plo_ref, mul_ref)
    wa = lo[:, 0:LANES]
    gl = lo[:, LANES:2 * LANES]

    w_in = w0_ref[...] + jnp.dot(jnp.tanh(wa).astype(BF16), w2_ref[...], preferred_element_type=F32)
    sp = jnp.maximum(-w_in, 0.0) + jnp.log(1.0 + jnp.exp(-jnp.abs(w_in)))
    w = -sp - 0.5
    lw = -jnp.exp(w)
    a = jax.nn.sigmoid(a0_ref[...] + jnp.dot(wa.astype(BF16), a2_ref[...], preferred_element_type=F32))
    g = jnp.dot(jax.nn.sigmoid(gl).astype(BF16), g2_ref[...], preferred_element_type=F32)

    ones_bd = _head_ones()
    kk = k * kk_ref[...]
    kk = kk * lax.rsqrt(jnp.maximum(_head_sum(kk * kk, ones_bd), 1e-24))
    k2 = k * (1.0 + (a - 1.0) * ka_ref[...])
    bonus = _head_sum(r * k2 * rk_ref[...], ones_bd) * v

    r_out[0] = r
    k_out[0] = k2
    v_out[0] = v
    lw_out[0] = lw
    kk_out[0] = kk
    a_out[0] = a
    bonus_out[0] = bonus
    g_out[0] = g


def _rwkv_prep(z, p, *, zb_col, ts):
    b, s, _ = z.shape
    bw = p["w0"].shape[-1]
    lw = 2 * LANES
    cb = zb_col // bw
    lb = (zb_col + 3 * bw) // lw
    assert zb_col % bw == 0 and (zb_col + 3 * bw) % lw == 0 and s % ts == 0 and ts % SUBLANES == 0
    sub = ts // SUBLANES

    def cur(col, width):
        return pl.BlockSpec((1, ts, width), lambda bi, t: (bi, t, col))

    def prev(col, width):
        return pl.BlockSpec((1, SUBLANES, width), lambda bi, t: (bi, jnp.maximum(t * sub - 1, 0), col))

    def vec(width):
        return pl.BlockSpec((1, width), lambda bi, t: (0, 0))

    def mat(width):
        return pl.BlockSpec((LANES, width), lambda bi, t: (0, 0))

    out_spec = pl.BlockSpec((1, ts, bw), lambda bi, t: (bi, t, 0))
    out_shape = jax.ShapeDtypeStruct((b, s, bw), F32)
    return pl.pallas_call(
        _prep_body,
        grid=(b, s // ts),
        in_specs=[cur(cb, bw), cur(cb + 1, bw), cur(cb + 2, bw), cur(lb, lw),
                  prev(cb, bw), prev(cb + 1, bw), prev(cb + 2, bw), prev(lb, lw),
                  vec(bw), vec(bw), vec(bw), vec(lw), vec(bw), vec(bw), vec(bw), vec(bw), vec(bw),
                  mat(bw), mat(bw), mat(bw)],
        out_specs=[out_spec] * 8,
        out_shape=[out_shape] * 8,
        compiler_params=_params("parallel", "arbitrary"),
        name="rwkv_prep",
    )(z, z, z, z, z, z, z, z,
      p["mu_r"], p["mu_k"], p["mu_v"], p["mu_l"], p["w0"], p["a0"], p["k_k"], p["k_a"], p["r_k"],
      p["w2"], p["a2"], p["g2"])


def _bdot(a, b):
    return jnp.dot(a.astype(BF16), b.astype(BF16), preferred_element_type=F32)


def _scan_body(r_ref, k_ref, v_ref, lw_ref, kk_ref, a_ref, bonus_ref, g_ref, gng_ref, gnb_ref,
               o_ref, h_ref, *, chunks, pairs):
    c_len = RWKV_CHUNK
    hd = B_HEAD_DIM
    prs = range(pairs)
    lanes = [slice(p * LANES, (p + 1) * LANES) for p in prs]

    @pl.when(pl.program_id(1) == 0)
    def _():
        h_ref[...] = jnp.zeros_like(h_ref)

    lane = lax.broadcasted_iota(jnp.int32, (c_len, LANES), 1)
    rowc = lax.broadcasted_iota(jnp.int32, (c_len, LANES), 0)
    head0 = lane < hd
    scol = lane % hd
    strict = scol < rowc
    incl = scol <= rowc
    r2 = lax.broadcasted_iota(jnp.int32, (LANES, LANES), 0)
    c2 = lax.broadcasted_iota(jnp.int32, (LANES, LANES), 1)
    bd_mask = (r2 // hd) == (c2 // hd)
    eye = jnp.where(r2 == c2, 1.0, 0.0)
    tri = jnp.where(lax.broadcasted_iota(jnp.int32, (c_len, c_len), 1)
                    <= lax.broadcasted_iota(jnp.int32, (c_len, c_len), 0), 1.0, 0.0)
    ones_bd = _head_ones()
    gng = [gng_ref[:, ln] for ln in lanes]
    gnb = [gnb_ref[:, ln] for ln in lanes]

    def stack(x):
        return jnp.concatenate([jnp.where(head0, x, 0.0), jnp.where(head0, 0.0, x)], axis=0)

    def chunk_step(c, carry):
        rows = pl.ds(pl.multiple_of(c * c_len, c_len), c_len)

        def load(ref):
            return [ref[0, rows, ln] for ln in lanes]

        r, k, v, lw, kk, a = load(r_ref), load(k_ref), load(v_ref), load(lw_ref), load(kk_ref), load(a_ref)
        cum = [jnp.dot(tri, x, precision=lax.Precision.HIGHEST, preferred_element_type=F32) for x in lw]
        p_inv = [jnp.exp(-x) for x in cum]
        to_end = [jnp.exp(x[c_len - 1:c_len, :] - x) for x in cum]
        at = [-kk[p] * jnp.exp(cum[p] - lw[p]) for p in prs]
        rp = [r[p] * jnp.exp(cum[p]) for p in prs]
        b = [kk[p] * a[p] for p in prs]
        bm = [b[p] * p_inv[p] for p in prs]
        km = [k[p] * p_inv[p] for p in prs]
        bc_t = [(b[p] * to_end[p]).T for p in prs]
        kc_t = [(k[p] * to_end[p]).T for p in prs]
        pc_col = [jnp.exp(x.T[:, c_len - 1:c_len]) for x in cum]

        lhs = [jnp.concatenate([at[p], rp[p]], axis=0).astype(BF16) for p in prs]
        rhs = [jnp.concatenate([stack(bm[p]), stack(km[p])], axis=0).astype(BF16) for p in prs]
        amat = [lax.dot_general(lhs[p], rhs[p], _NT, preferred_element_type=F32) for p in prs]
        a_ab = [jnp.where(strict, m[0:c_len, 0:LANES], 0.0) for m in amat]
        a_ak = [jnp.where(strict, m[0:c_len, LANES:2 * LANES], 0.0) for m in amat]
        a_rb = [jnp.where(incl, m[c_len:, 0:LANES], 0.0) for m in amat]
        a_rk = [jnp.where(incl, m[c_len:, LANES:2 * LANES], 0.0) for m in amat]

        pw = [stack(x) for x in a_ab]
        t_inv = [eye + x for x in pw]
        for _ in range(int(math.log2(c_len)) - 1):
            pw = [_bdot(x, x) for x in pw]
            t_inv = [t_inv[p] + _bdot(t_inv[p], pw[p]) for p in prs]

        v_st = [stack(x) for x in v]
        akv = [_bdot(stack(a_ak[p]), v_st[p]) for p in prs]
        sol = [_bdot(t_inv[p], jnp.concatenate([stack(at[p]), akv[p]], axis=1)) for p in prs]
        y0 = [_bdot(a_rk[p], v_st[p]) for p in prs]
        hkv = [jnp.where(bd_mask, _bdot(kc_t[p], v[p]), 0.0) for p in prs]

        h = [h_ref[p] for p in prs]
        hb = [x.astype(BF16) for x in h]
        u_st = [jnp.dot(sol[p][:, 0:LANES].astype(BF16), hb[p], preferred_element_type=F32)
                + sol[p][:, LANES:] for p in prs]
        y = [jnp.dot(rp[p].astype(BF16), hb[p], preferred_element_type=F32) + y0[p] for p in prs]
        y = [y[p] + _bdot(a_rb[p], u_st[p]) for p in prs]
        u = [x[0:c_len] + x[c_len:] for x in u_st]
        for p in prs:
            h_ref[p] = pc_col[p] * h[p] + jnp.where(bd_mask, _bdot(bc_t[p], u[p]), 0.0) + hkv[p]

        mean = [_split_dot(x, ones_bd) * (1.0 / hd) for x in y]
        dlt = [y[p] - mean[p] for p in prs]
        var = [_split_dot(x * x, ones_bd) * (1.0 / hd) for x in dlt]
        for p in prs:
            yn = dlt[p] * lax.rsqrt(var[p] + GN_EPS) * gng[p] + gnb[p]
            o_ref[0, rows, lanes[p]] = (yn + bonus_ref[0, rows, lanes[p]]) * g_ref[0, rows, lanes[p]]
        return carry

    lax.fori_loop(0, chunks, chunk_step, 0)


def _rwkv_scan(prep, gn_g, gn_b, *, ts):
    b, s, bw = prep[0].shape
    assert ts % RWKV_CHUNK == 0 and s % ts == 0 and bw % LANES == 0
    pairs = bw // LANES
    tok = pl.BlockSpec((1, ts, bw), lambda bi, t: (bi, t, 0))
    vec = pl.BlockSpec((1, bw), lambda bi, t: (0, 0))
    return pl.pallas_call(
        functools.partial(_scan_body, chunks=ts // RWKV_CHUNK, pairs=pairs),
        grid=(b, s // ts),
        in_specs=[tok] * 8 + [vec, vec],
        out_specs=tok,
        out_shape=jax.ShapeDtypeStruct((b, s, bw), F32),
        scratch_shapes=[pltpu.VMEM((pairs, LANES, LANES), F32)],
        compiler_params=_params("parallel", "arbitrary"),
        name="rwkv_scan",
    )(*prep, gn_g, gn_b)


def _outproj_body(h_ref, ya_ref, yb_ref, wa_ref, wb_ref, o_ref):
    acc = jnp.dot(ya_ref[...].astype(BF16), wa_ref[...], preferred_element_type=F32)
    acc = acc + jnp.dot(yb_ref[...].astype(BF16), wb_ref[...], preferred_element_type=F32)
    o_ref[...] = h_ref[...] + acc


def _outproj(h, ya, yb, w_out, e, *, tm, tn):
    t, d = h.shape
    wa, wb = ya.shape[-1], yb.shape[-1]
    assert wa == wb
    return pl.pallas_call(
        _outproj_body,
        grid=(t // tm, d // tn),
        in_specs=[
            pl.BlockSpec((tm, tn), lambda i, j: (i, j)),
            pl.BlockSpec((tm, wa), lambda i, j: (i, 0)),
            pl.BlockSpec((tm, wb), lambda i, j: (i, 0)),
            pl.BlockSpec((None, wa, tn), lambda i, j: (e, 0, j)),
            pl.BlockSpec((None, wb, tn), lambda i, j: (e, 1, j)),
        ],
        out_specs=pl.BlockSpec((tm, tn), lambda i, j: (i, j)),
        out_shape=jax.ShapeDtypeStruct((t, d), F32),
        compiler_params=_params("parallel", "arbitrary"),
        name="outproj",
    )(h, ya, yb, w_out, w_out)


def _pool_body(h_ref, halo_ref, g_ref, w_ref, sc_ref, o_ref, *, windows, halo):
    ts, d = h_ref.shape[1], h_ref.shape[2]
    grp = d // len(windows)
    t = pl.program_id(1)
    hcur = h_ref[0]
    u = _rms(hcur, g_ref[...])
    uh = jnp.where(t == 0, 0.0, _rms(halo_ref[0], g_ref[...]))
    pos = t * ts + lax.broadcasted_iota(jnp.int32, (ts, 1), 0)
    outs = []
    for gi, win in enumerate(windows):
        cols = slice(gi * grp, (gi + 1) * grp)
        ug = u[:, cols]
        ext = jnp.concatenate([uh[:, cols], ug], axis=0)
        span = 1
        while span < win:
            ext = ext + pltpu.roll(ext, span, axis=0)
            span *= 2
        cnt = jnp.minimum(pos + 1, win).astype(F32)
        diff = ext[halo:, :] / cnt - ug
        outs.append(jnp.dot(diff.astype(BF16), w_ref[gi], preferred_element_type=F32))
    o_ref[0] = hcur + jnp.concatenate(outs, axis=-1) * sc_ref[...]


def _pool(h, gain, w_grp, o, scale, *, ts):
    b, s, d = h.shape
    halo = 2 * SUBLANES
    assert max(POOL_WINDOWS) <= halo and all(w & (w - 1) == 0 for w in POOL_WINDOWS)
    assert s % ts == 0 and ts % halo == 0
    ng, grp = w_grp.shape[1], w_grp.shape[2]
    sub = ts // halo
    return pl.pallas_call(
        functools.partial(_pool_body, windows=POOL_WINDOWS, halo=halo),
        grid=(b, s // ts),
        in_specs=[
            pl.BlockSpec((1, ts, d), lambda bi, t: (bi, t, 0)),
            pl.BlockSpec((1, halo, d), lambda bi, t: (bi, jnp.maximum(t * sub - 1, 0), 0)),
            pl.BlockSpec((1, d), lambda bi, t: (0, 0)),
            pl.BlockSpec((None, ng, grp, grp), lambda bi, t: (o, 0, 0, 0)),
            pl.BlockSpec((1, d), lambda bi, t: (0, 0)),
        ],
        out_specs=pl.BlockSpec((1, ts, d), lambda bi, t: (bi, t, 0)),
        out_shape=jax.ShapeDtypeStruct((b, s, d), F32),
        compiler_params=_params("parallel", "arbitrary"),
        name="pool",
    )(h, h, gain, w_grp, scale)


def _pick(n, pref):
    for c in range(min(pref, n), 0, -1):
        if n % c == 0 and (c % SUBLANES == 0 or c == n):
            return c
    return n


def kernel(x, ffn1_norm, ffn1_wg, ffn1_wu, ffn1_wd, mix_norm, ffn2_norm, ffn2_wg, ffn2_wu, ffn2_wd,
           ab_w_in, ab_w_out, rwkv_mu, rwkv_w0, rwkv_w2, rwkv_a0, rwkv_a2, rwkv_g2, rwkv_k_k, rwkv_k_a,
           rwkv_r_k, rwkv_gn_g, rwkv_gn_b, pool_w, pool_scale, final_norm):
    b, s, d = x.shape
    t = b * s
    depth = ffn1_norm.shape[0]
    bw = rwkv_w0.shape[-1]
    aw = ab_w_out.shape[1] - bw
    a_heads = aw // A_HEAD_DIM
    zb_col = 3 * aw
    assert rwkv_w2.shape[1] == LORA and 3 * LORA <= 2 * LANES

    w1g, w1u, w1d = ffn1_wg.astype(BF16), ffn1_wu.astype(BF16), ffn1_wd.astype(BF16)
    w2g, w2u, w2d = ffn2_wg.astype(BF16), ffn2_wu.astype(BF16), ffn2_wd.astype(BF16)
    n_in = ab_w_in.shape[-1]
    n_pad = zb_col + 3 * bw + 2 * LANES - n_in
    w_in = jnp.pad(ab_w_in, ((0, 0), (0, 0), (0, n_pad))).astype(BF16)
    w_out = ab_w_out.astype(BF16)
    pool_wb = pool_w.astype(BF16)
    zrow = jnp.zeros((LANES - LORA, bw), F32)

    tm = _pick(t, 1024)
    tmo = _pick(t, 512)
    tf = _pick(w1g.shape[-1], 512)
    row = lambda v: v.reshape(1, -1)

    h = x.reshape(t, d)
    for layer in range(depth):
        h = _ffn(h, row(ffn1_norm[layer]), w1g, w1u, w1d, layer, row(final_norm),
                 final=False, tm=tm, tf=tf)
        if layer % 2 == 0:
            e = layer // 2
            z = _inproj(h, row(mix_norm[layer]), w_in, e, tm=tm, tn=_pick(w_in.shape[-1], 1280))
            z = z.reshape(b, s, -1)
            ya = _moba(z, heads=a_heads, hg=math.gcd(a_heads, MOBA_HEAD_GROUP))
            mu = rwkv_mu[e]
            prm = {
                "mu_r": row(mu[0:bw]), "mu_k": row(mu[bw:2 * bw]), "mu_v": row(mu[2 * bw:3 * bw]),
                "mu_l": row(jnp.pad(mu[3 * bw:], (0, n_pad))),
                "w0": row(rwkv_w0[e]), "a0": row(rwkv_a0[e]), "k_k": row(rwkv_k_k[e]),
                "k_a": row(rwkv_k_a[e]), "r_k": row(rwkv_r_k[e]),
                "w2": jnp.concatenate([rwkv_w2[e], zrow], axis=0).astype(BF16),
                "a2": jnp.concatenate([zrow, rwkv_a2[e]], axis=0).astype(BF16),
                "g2": jnp.concatenate([rwkv_g2[e], zrow], axis=0).astype(BF16),
            }
            prep = _rwkv_prep(z, prm, zb_col=zb_col, ts=_pick(s, 256))
            yb = _rwkv_scan(prep, row(rwkv_gn_g[e]), row(rwkv_gn_b[e]), ts=_pick(s, 256))
            h = _outproj(h, ya.reshape(t, aw), yb.reshape(t, bw), w_out, e, tm=tmo, tn=d)
        else:
            o = layer // 2
            h = _pool(h.reshape(b, s, d), row(mix_norm[layer]), pool_wb, o, row(pool_scale[o]),
                      ts=_pick(s, 512)).reshape(t, d)
        h = _ffn(h, row(ffn2_norm[layer]), w2g, w2u, w2d, layer, row(final_norm),
                 final=(layer == depth - 1), tm=tm, tf=tf)
    return h.reshape(b, s, d)
```

```python
import functools
import math

import jax
import jax.numpy as jnp
from jax import lax
from jax.experimental import pallas as pl
from jax.experimental.pallas import tpu as pltpu

F32 = jnp.float32
BF16 = jnp.bfloat16

A_HEAD_DIM = 128
MOBA_BLOCK = 256
MOBA_TOPK = 3
B_HEAD_DIM = 64
LORA = 64
GN_EPS = 64e-5
RMS_EPS = 1e-6
POOL_WINDOWS = (2, 4, 8, 16)

LANES = 128
SUBLANES = 8
VMEM_LIMIT_BYTES = 56 * 1024 * 1024

NEG_BIG = -1e30
MOBA_HEAD_GROUP = 4
RWKV_UNROLL = 2
RWKV_CHUNK = 64

_NT = (((1,), (1,)), ((), ()))


def _rms(x, gain):
    ms = jnp.mean(x * x, axis=-1, keepdims=True)
    return x * lax.rsqrt(ms + RMS_EPS) * gain


def _params(*sem):
    return pltpu.CompilerParams(dimension_semantics=sem, vmem_limit_bytes=VMEM_LIMIT_BYTES)


def _ffn_body(h_ref, g_ref, wg_ref, wu_ref, wd_ref, fg_ref, o_ref, n_ref, *, final):
    f = pl.program_id(1)

    @pl.when(f == 0)
    def _():
        h = h_ref[...]
        n_ref[...] = _rms(h, g_ref[...]).astype(BF16)
        o_ref[...] = h

    n = n_ref[...]
    gate = jnp.dot(n, wg_ref[...], preferred_element_type=F32)
    up = jnp.dot(n, wu_ref[...], preferred_element_type=F32)
    act = (gate * jax.nn.sigmoid(gate) * up).astype(BF16)
    o_ref[...] += 0.5 * jnp.dot(act, wd_ref[...], preferred_element_type=F32)

    if final:
        @pl.when(f == pl.num_programs(1) - 1)
        def _():
            o_ref[...] = _rms(o_ref[...], fg_ref[...])


def _ffn(h, gain, wg, wu, wd, layer, final_gain, *, final, tm, tf):
    t, d = h.shape
    f = wg.shape[-1]
    return pl.pallas_call(
        functools.partial(_ffn_body, final=final),
        grid=(t // tm, f // tf),
        in_specs=[
            pl.BlockSpec((tm, d), lambda i, j: (i, 0)),
            pl.BlockSpec((1, d), lambda i, j: (0, 0)),
            pl.BlockSpec((None, d, tf), lambda i, j: (layer, 0, j)),
            pl.BlockSpec((None, d, tf), lambda i, j: (layer, 0, j)),
            pl.BlockSpec((None, tf, d), lambda i, j: (layer, j, 0)),
            pl.BlockSpec((1, d), lambda i, j: (0, 0)),
        ],
        out_specs=pl.BlockSpec((tm, d), lambda i, j: (i, 0)),
        out_shape=jax.ShapeDtypeStruct((t, d), F32),
        scratch_shapes=[pltpu.VMEM((tm, d), BF16)],
        compiler_params=_params("parallel", "arbitrary"),
        name="ffn",
    )(h, gain, wg, wu, wd, final_gain)


def _inproj_body(h_ref, g_ref, w_ref, z_ref, n_ref):
    @pl.when(pl.program_id(1) == 0)
    def _():
        n_ref[...] = _rms(h_ref[...], g_ref[...]).astype(BF16)

    z_ref[...] = jnp.dot(n_ref[...], w_ref[...], preferred_element_type=F32)


def _inproj(h, gain, w_in, e, *, tm, tn):
    t, d = h.shape
    n = w_in.shape[-1]
    return pl.pallas_call(
        _inproj_body,
        grid=(t // tm, n // tn),
        in_specs=[
            pl.BlockSpec((tm, d), lambda i, j: (i, 0)),
            pl.BlockSpec((1, d), lambda i, j: (0, 0)),
            pl.BlockSpec((None, d, tn), lambda i, j: (e, 0, j)),
        ],
        out_specs=pl.BlockSpec((tm, tn), lambda i, j: (i, j)),
        out_shape=jax.ShapeDtypeStruct((t, n), F32),
        scratch_shapes=[pltpu.VMEM((tm, d), BF16)],
        compiler_params=_params("parallel", "arbitrary"),
        name="inproj",
    )(h, gain, w_in)


def _moba_body(q_ref, k_ref, v_ref, o_ref, kb_ref, vt_ref, km_ref, sel_ref, s0_ref, s1_ref, acc_ref,
               *, hg, nb, blk, topk, scale):
    i = pl.program_id(2)
    dh = A_HEAD_DIM
    cols = [slice(hh * dh, (hh + 1) * dh) for hh in range(hg)]

    @pl.when(i == 0)
    def _():
        km_ref[...] = jnp.zeros_like(km_ref)
        for hh in range(hg):
            for jb in range(nb):
                rows = slice(jb * blk, (jb + 1) * blk)
                kf = k_ref[0, rows, cols[hh]]
                kb_ref[hh, jb] = kf.astype(BF16)
                vt_ref[hh, jb] = v_ref[0, rows, cols[hh]].T.astype(BF16)
                km_ref[hh, jb:jb + 1, :] = jnp.mean(kf, axis=0, keepdims=True)

    qs = [q_ref[0, :, c] for c in cols]
    for hh in range(hg):
        gate = lax.dot_general(km_ref[hh], qs[hh], _NT, precision=lax.Precision.HIGHEST,
                               preferred_element_type=F32)
        blk_id = lax.broadcasted_iota(jnp.int32, gate.shape, 0)
        rank = jnp.zeros(gate.shape, jnp.int32)
        for jp in range(nb):
            row = gate[jp:jp + 1, :]
            beats = (row > gate) | ((row == gate) & (jp < blk_id))
            rank = rank + jnp.where(beats & (jp < i), 1, 0)
        sel_ref[hh] = jnp.where((blk_id < i) & (rank < topk), 0.0, NEG_BIG)

    qb = [(q * scale).astype(BF16) for q in qs]

    def scores(hh, j):
        return lax.dot_general(kb_ref[hh, j], qb[hh], _NT, preferred_element_type=F32)

    def pv(hh, p, j):
        return jnp.dot(vt_ref[hh, j], p.astype(BF16), preferred_element_type=F32)

    kpos = lax.broadcasted_iota(jnp.int32, (blk, blk), 0)
    qpos = lax.broadcasted_iota(jnp.int32, (blk, blk), 1)
    causal = kpos <= qpos
    heads = range(hg)
    ss = [jnp.where(causal, scores(hh, i), NEG_BIG) for hh in heads]
    ms = [jnp.max(s, axis=0, keepdims=True) for s in ss]
    ps = [jnp.exp(s - m) for s, m in zip(ss, ms)]
    ls = [jnp.sum(p, axis=0, keepdims=True) for p in ps]
    accs = [pv(hh, ps[hh], i) for hh in heads]

    for hh in heads:
        acc_ref[hh] = accs[hh]

    def produce(j, s_ref):
        mx = []
        for hh in heads:
            s = scores(hh, jnp.minimum(j, nb - 1))
            s_ref[hh] = s
            mx.append(jnp.max(s, axis=0, keepdims=True))
        return tuple(mx)

    def consume(j, s_ref, ms, ls, mx):
        sel = [sel_ref[hh, pl.ds(j, 1), :] for hh in heads]
        m_new = [jnp.maximum(m, x + b) for m, x, b in zip(ms, mx, sel)]
        alpha = [jnp.exp(m - mn) for m, mn in zip(ms, m_new)]
        shift = [mn - 2.0 * b for mn, b in zip(m_new, sel)]
        ps = [jnp.exp(s_ref[hh] - shift[hh]) for hh in heads]
        ls = [a * l + jnp.sum(p, axis=0, keepdims=True) for a, l, p in zip(alpha, ls, ps)]
        pvs = [pv(hh, ps[hh], j) for hh in heads]
        for hh in heads:
            acc_ref[hh] = alpha[hh] * acc_ref[hh] + pvs[hh]
        return tuple(m_new), tuple(ls)

    def past(jj, carry):
        ms, ls, mx0 = carry
        j0 = 2 * jj
        mx1 = produce(j0 + 1, s1_ref)
        ms, ls = consume(j0, s0_ref, ms, ls, mx0)
        mx0 = produce(j0 + 2, s0_ref)
        ms, ls = consume(j0 + 1, s1_ref, ms, ls, mx1)
        return ms, ls, mx0

    mx0 = produce(0, s0_ref)
    _, ls, _ = lax.fori_loop(0, (i + 1) // 2, past, (tuple(ms), tuple(ls), mx0))
    for hh in heads:
        o_ref[0, :, cols[hh]] = (acc_ref[hh] / ls[hh]).T


def _moba(z, *, heads, hg):
    b, s, _ = z.shape
    dh, blk = A_HEAD_DIM, MOBA_BLOCK
    nb = s // blk
    nbp = -(-nb // SUBLANES) * SUBLANES
    ng = heads // hg
    assert s % blk == 0 and heads % hg == 0
    return pl.pallas_call(
        functools.partial(_moba_body, hg=hg, nb=nb, blk=blk, topk=min(MOBA_TOPK, nb), scale=dh ** -0.5),
        grid=(b, ng, nb),
        in_specs=[
            pl.BlockSpec((1, blk, hg * dh), lambda bi, g, i: (bi, i, g)),
            pl.BlockSpec((1, s, hg * dh), lambda bi, g, i: (bi, 0, ng + g)),
            pl.BlockSpec((1, s, hg * dh), lambda bi, g, i: (bi, 0, 2 * ng + g)),
        ],
        out_specs=pl.BlockSpec((1, blk, hg * dh), lambda bi, g, i: (bi, i, g)),
        out_shape=jax.ShapeDtypeStruct((b, s, heads * dh), F32),
        scratch_shapes=[pltpu.VMEM((hg, nb, blk, dh), BF16), pltpu.VMEM((hg, nb, dh, blk), BF16),
                        pltpu.VMEM((hg, nbp, dh), F32), pltpu.VMEM((hg, nbp, blk), F32),
                        pltpu.VMEM((hg, blk, blk), F32), pltpu.VMEM((hg, blk, blk), F32),
                        pltpu.VMEM((hg, dh, blk), F32)],
        compiler_params=_params("parallel", "parallel", "arbitrary"),
        name="moba",
    )(z, z, z)


def _split_dot(x, ones_bd):
    hi = x.astype(BF16)
    lo = (x - hi.astype(F32)).astype(BF16)
    return (jnp.dot(hi, ones_bd, preferred_element_type=F32)
            + jnp.dot(lo, ones_bd, preferred_element_type=F32))


def _head_sum(x, ones_bd):
    cols = [_split_dot(x[:, c:c + LANES], ones_bd) for c in range(0, x.shape[-1], LANES)]
    return cols[0] if len(cols) == 1 else jnp.concatenate(cols, axis=-1)


def _head_ones():
    r = lax.broadcasted_iota(jnp.int32, (LANES, LANES), 0) // B_HEAD_DIM
    c = lax.broadcasted_iota(jnp.int32, (LANES, LANES), 1) // B_HEAD_DIM
    return jnp.where(r == c, 1.0, 0.0).astype(BF16)


def _prep_body(zr_ref, zk_ref, zv_ref, zl_ref, pr_ref, pk_ref, pv_ref, plo_ref,
               mur_ref, muk_ref, muv_ref, mul_ref, w0_ref, a0_ref, kk_ref, ka_ref, rk_ref,
               w2_ref, a2_ref, g2_ref,
               r_out, k_out, v_out, lw_out, kk_out, a_out, bonus_out, g_out):
    ts = zr_ref.shape[1]
    first = pl.program_id(1) == 0
    row = lax.broadcasted_iota(jnp.int32, (ts, 1), 0)

    def lerp(z_ref, prev_ref, mu_ref):
        x = z_ref[0]
        prev = jnp.where(first, 0.0, prev_ref[0, SUBLANES - 1:SUBLANES, :])
        shifted = jnp.where(row == 0, prev, pltpu.roll(x, 1, axis=0))
        return x + (shifted - x) * mu_ref[...]

    r = lerp(zr_ref, pr_ref, mur_ref)
    k = lerp(zk_ref, pk_ref, muk_ref)
    v = lerp(zv_ref, pv_ref, muv_ref)
    lo = lerp(zl_ref, plo_ref, mul_ref)
    wa = lo[:, 0:LANES]
    gl = lo[:, LANES:2 * LANES]

    w_in = w0_ref[...] + jnp.dot(jnp.tanh(wa).astype(BF16), w2_ref[...], preferred_element_type=F32)
    sp = jnp.maximum(-w_in, 0.0) + jnp.log(1.0 + jnp.exp(-jnp.abs(w_in)))
    w = -sp - 0.5
    lw = -jnp.exp(w)
    a = jax.nn.sigmoid(a0_ref[...] + jnp.dot(wa.astype(BF16), a2_ref[...], preferred_element_type=F32))
    g = jnp.dot(jax.nn.sigmoid(gl).astype(BF16), g2_ref[...], preferred_element_type=F32)

    ones_bd = _head_ones()
    kk = k * kk_ref[...]
    kk = kk * lax.rsqrt(jnp.maximum(_head_sum(kk * kk, ones_bd), 1e-24))
    k2 = k * (1.0 + (a - 1.0) * ka_ref[...])
    bonus = _head_sum(r * k2 * rk_ref[...], ones_bd) * v

    r_out[0] = r
    k_out[0] = k2
    v_out[0] = v
    lw_out[0] = lw
    kk_out[0] = kk
    a_out[0] = a
    bonus_out[0] = bonus
    g_out[0] = g


def _rwkv_prep(z, p, *, zb_col, ts):
    b, s, _ = z.shape
    bw = p["w0"].shape[-1]
    lw = 2 * LANES
    cb = zb_col // bw
    lb = (zb_col + 3 * bw) // lw
    assert zb_col % bw == 0 and (zb_col + 3 * bw) % lw == 0 and s % ts == 0 and ts % SUBLANES == 0
    sub = ts // SUBLANES

    def cur(col, width):
        return pl.BlockSpec((1, ts, width), lambda bi, t: (bi, t, col))

    def prev(col, width):
        return pl.BlockSpec((1, SUBLANES, width), lambda bi, t: (bi, jnp.maximum(t * sub - 1, 0), col))

    def vec(width):
        return pl.BlockSpec((1, width), lambda bi, t: (0, 0))

    def mat(width):
        return pl.BlockSpec((LANES, width), lambda bi, t: (0, 0))

    out_spec = pl.BlockSpec((1, ts, bw), lambda bi, t: (bi, t, 0))
    out_shape = jax.ShapeDtypeStruct((b, s, bw), F32)
    return pl.pallas_call(
        _prep_body,
        grid=(b, s // ts),
        in_specs=[cur(cb, bw), cur(cb + 1, bw), cur(cb + 2, bw), cur(lb, lw),
                  prev(cb, bw), prev(cb + 1, bw), prev(cb + 2, bw), prev(lb, lw),
                  vec(bw), vec(bw), vec(bw), vec(lw), vec(bw), vec(bw), vec(bw), vec(bw), vec(bw),
                  mat(bw), mat(bw), mat(bw)],
        out_specs=[out_spec] * 8,
        out_shape=[out_shape] * 8,
        compiler_params=_params("parallel", "arbitrary"),
        name="rwkv_prep",
    )(z, z, z, z, z, z, z, z,
      p["mu_r"], p["mu_k"], p["mu_v"], p["mu_l"], p["w0"], p["a0"], p["k_k"], p["k_a"], p["r_k"],
      p["w2"], p["a2"], p["g2"])


def _bdot(a, b):
    return jnp.dot(a.astype(BF16), b.astype(BF16), preferred_element_type=F32)


def _scan_body(r_ref, k_ref, v_ref, lw_ref, kk_ref, a_ref, bonus_ref, g_ref, gng_ref, gnb_ref,
               o_ref, h_ref, *, chunks, pairs, unroll):
    c_len = RWKV_CHUNK
    hd = B_HEAD_DIM
    prs = range(pairs)
    lanes = [slice(p * LANES, (p + 1) * LANES) for p in prs]

    @pl.when(pl.program_id(1) == 0)
    def _():
        h_ref[...] = jnp.zeros_like(h_ref)

    lane = lax.broadcasted_iota(jnp.int32, (c_len, LANES), 1)
    rowc = lax.broadcasted_iota(jnp.int32, (c_len, LANES), 0)
    head0 = lane < hd
    scol = lane % hd
    strict = scol < rowc
    incl = scol <= rowc
    r2 = lax.broadcasted_iota(jnp.int32, (LANES, LANES), 0)
    c2 = lax.broadcasted_iota(jnp.int32, (LANES, LANES), 1)
    bd_mask = (r2 // hd) == (c2 // hd)
    eye_cat = jnp.where(scol == rowc, 1.0, 0.0)
    tri = jnp.where(lax.broadcasted_iota(jnp.int32, (c_len, c_len), 1)
                    <= lax.broadcasted_iota(jnp.int32, (c_len, c_len), 0), 1.0, 0.0)
    ones_bd = _head_ones()
    gng = [gng_ref[:, ln] for ln in lanes]
    gnb = [gnb_ref[:, ln] for ln in lanes]

    def stack(x):
        return jnp.concatenate([jnp.where(head0, x, 0.0), jnp.where(head0, 0.0, x)], axis=0)

    def chunk_step(c, carry):
        rows = [pl.ds(pl.multiple_of((c * unroll + ci) * c_len, c_len), c_len) for ci in range(unroll)]
        units = [(ci, p) for ci in range(unroll) for p in prs]
        us = range(len(units))

        def load(ref):
            return [ref[0, rows[ci], lanes[p]] for ci, p in units]

        r, k, v, lw, kk, a = load(r_ref), load(k_ref), load(v_ref), load(lw_ref), load(kk_ref), load(a_ref)
        cum = [jnp.dot(tri, x, precision=lax.Precision.HIGHEST, preferred_element_type=F32) for x in lw]
        p_inv = [jnp.exp(-x) for x in cum]
        to_end = [jnp.exp(x[c_len - 1:c_len, :] - x) for x in cum]
        at = [-kk[u] * jnp.exp(cum[u] - lw[u]) for u in us]
        rp = [r[u] * jnp.exp(cum[u]) for u in us]
        b = [kk[u] * a[u] for u in us]
        bm = [b[u] * p_inv[u] for u in us]
        km = [k[u] * p_inv[u] for u in us]
        bc_t = [(b[u] * to_end[u]).T for u in us]
        kc_t = [(k[u] * to_end[u]).T for u in us]
        pc_col = [jnp.exp(x.T[:, c_len - 1:c_len]) for x in cum]

        lhs = [jnp.concatenate([at[u], rp[u]], axis=0).astype(BF16) for u in us]
        rhs = [jnp.concatenate([stack(bm[u]), stack(km[u])], axis=0).astype(BF16) for u in us]
        amat = [lax.dot_general(lhs[u], rhs[u], _NT, preferred_element_type=F32) for u in us]
        a_ab = [jnp.where(strict, m[0:c_len, 0:LANES], 0.0) for m in amat]
        a_ak = [jnp.where(strict, m[0:c_len, LANES:2 * LANES], 0.0) for m in amat]
        a_rb = [jnp.where(incl, m[c_len:, 0:LANES], 0.0) for m in amat]
        a_rk = [jnp.where(incl, m[c_len:, LANES:2 * LANES], 0.0) for m in amat]

        pw = a_ab
        t_inv = [eye_cat + x for x in a_ab]
        pw_bd = [stack(x) for x in pw]
        pw = [_bdot(pw[u], pw_bd[u]) for u in us]
        for _ in range(int(math.log2(c_len)) - 2):
            pw_bd = [stack(x) for x in pw]
            both = [_bdot(jnp.concatenate([pw[u], t_inv[u]], axis=0), pw_bd[u]) for u in us]
            pw = [x[0:c_len] for x in both]
            t_inv = [t_inv[u] + both[u][c_len:] for u in us]
        pw_bd = [stack(x) for x in pw]
        t_inv = [t_inv[u] + _bdot(t_inv[u], pw_bd[u]) for u in us]

        v_st = [stack(x) for x in v]
        akv = [_bdot(a_ak[u], v_st[u]) for u in us]
        sol = [_bdot(t_inv[u], jnp.concatenate([stack(at[u]), stack(akv[u])], axis=1)) for u in us]
        y0 = [_bdot(a_rk[u], v_st[u]) for u in us]
        hkv = [jnp.where(bd_mask, _bdot(kc_t[u], v[u]), 0.0) for u in us]

        h = [h_ref[p] for p in prs]
        ys = []
        for ci in range(unroll):
            un = [ci * pairs + p for p in prs]
            hb = [x.astype(BF16) for x in h]
            uu = [jnp.dot(sol[un[p]][:, 0:LANES].astype(BF16), hb[p], preferred_element_type=F32)
                  + sol[un[p]][:, LANES:] for p in prs]
            y = [jnp.dot(rp[un[p]].astype(BF16), hb[p], preferred_element_type=F32) + y0[un[p]] for p in prs]
            ys += [y[p] + _bdot(a_rb[un[p]], stack(uu[p])) for p in prs]
            h = [pc_col[un[p]] * h[p] + jnp.where(bd_mask, _bdot(bc_t[un[p]], uu[p]), 0.0) + hkv[un[p]]
                 for p in prs]
        for p in prs:
            h_ref[p] = h[p]

        mean = [_split_dot(x, ones_bd) * (1.0 / hd) for x in ys]
        dlt = [ys[u] - mean[u] for u in us]
        var = [_split_dot(x * x, ones_bd) * (1.0 / hd) for x in dlt]
        for u, (ci, p) in enumerate(units):
            yn = dlt[u] * lax.rsqrt(var[u] + GN_EPS) * gng[p] + gnb[p]
            o_ref[0, rows[ci], lanes[p]] = (yn + bonus_ref[0, rows[ci], lanes[p]]) * g_ref[0, rows[ci], lanes[p]]
        return carry

    lax.fori_loop(0, chunks // unroll, chunk_step, 0)


def _rwkv_scan(prep, gn_g, gn_b, *, ts):
    b, s, bw = prep[0].shape
    assert ts % (RWKV_CHUNK * RWKV_UNROLL) == 0 and s % ts == 0 and bw % LANES == 0
    pairs = bw // LANES
    tok = pl.BlockSpec((1, ts, bw), lambda bi, t: (bi, t, 0))
    vec = pl.BlockSpec((1, bw), lambda bi, t: (0, 0))
    return pl.pallas_call(
        functools.partial(_scan_body, chunks=ts // RWKV_CHUNK, pairs=pairs, unroll=RWKV_UNROLL),
        grid=(b, s // ts),
        in_specs=[tok] * 8 + [vec, vec],
        out_specs=tok,
        out_shape=jax.ShapeDtypeStruct((b, s, bw), F32),
        scratch_shapes=[pltpu.VMEM((pairs, LANES, LANES), F32)],
        compiler_params=_params("parallel", "arbitrary"),
        name="rwkv_scan",
    )(*prep, gn_g, gn_b)


def _outproj_body(h_ref, ya_ref, yb_ref, wa_ref, wb_ref, o_ref):
    acc = jnp.dot(ya_ref[...].astype(BF16), wa_ref[...], preferred_element_type=F32)
    acc = acc + jnp.dot(yb_ref[...].astype(BF16), wb_ref[...], preferred_element_type=F32)
    o_ref[...] = h_ref[...] + acc


def _outproj(h, ya, yb, w_out, e, *, tm, tn):
    t, d = h.shape
    wa, wb = ya.shape[-1], yb.shape[-1]
    assert wa == wb
    return pl.pallas_call(
        _outproj_body,
        grid=(t // tm, d // tn),
        in_specs=[
            pl.BlockSpec((tm, tn), lambda i, j: (i, j)),
            pl.BlockSpec((tm, wa), lambda i, j: (i, 0)),
            pl.BlockSpec((tm, wb), lambda i, j: (i, 0)),
            pl.BlockSpec((None, wa, tn), lambda i, j: (e, 0, j)),
            pl.BlockSpec((None, wb, tn), lambda i, j: (e, 1, j)),
        ],
        out_specs=pl.BlockSpec((tm, tn), lambda i, j: (i, j)),
        out_shape=jax.ShapeDtypeStruct((t, d), F32),
        compiler_params=_params("parallel", "arbitrary"),
        name="outproj",
    )(h, ya, yb, w_out, w_out)


def _pool_body(h_ref, halo_ref, g_ref, w_ref, sc_ref, o_ref, *, windows, halo):
    ts, d = h_ref.shape[1], h_ref.shape[2]
    grp = d // len(windows)
    t = pl.program_id(1)
    hcur = h_ref[0]
    u = _rms(hcur, g_ref[...])
    uh = jnp.where(t == 0, 0.0, _rms(halo_ref[0], g_ref[...]))
    pos = t * ts + lax.broadcasted_iota(jnp.int32, (ts, 1), 0)
    outs = []
    for gi, win in enumerate(windows):
        cols = slice(gi * grp, (gi + 1) * grp)
        ug = u[:, cols]
        ext = jnp.concatenate([uh[:, cols], ug], axis=0)
        span = 1
        while span < win:
            ext = ext + pltpu.roll(ext, span, axis=0)
            span *= 2
        cnt = jnp.minimum(pos + 1, win).astype(F32)
        diff = ext[halo:, :] / cnt - ug
        outs.append(jnp.dot(diff.astype(BF16), w_ref[gi], preferred_element_type=F32))
    o_ref[0] = hcur + jnp.concatenate(outs, axis=-1) * sc_ref[...]


def _pool(h, gain, w_grp, o, scale, *, ts):
    b, s, d = h.shape
    halo = 2 * SUBLANES
    assert max(POOL_WINDOWS) <= halo and all(w & (w - 1) == 0 for w in POOL_WINDOWS)
    assert s % ts == 0 and ts % halo == 0
    ng, grp = w_grp.shape[1], w_grp.shape[2]
    sub = ts // halo
    return pl.pallas_call(
        functools.partial(_pool_body, windows=POOL_WINDOWS, halo=halo),
        grid=(b, s // ts),
        in_specs=[
            pl.BlockSpec((1, ts, d), lambda bi, t: (bi, t, 0)),
            pl.BlockSpec((1, halo, d), lambda bi, t: (bi, jnp.maximum(t * sub - 1, 0), 0)),
            pl.BlockSpec((1, d), lambda bi, t: (0, 0)),
            pl.BlockSpec((None, ng, grp, grp), lambda bi, t: (o, 0, 0, 0)),
            pl.BlockSpec((1, d), lambda bi, t: (0, 0)),
        ],
        out_specs=pl.BlockSpec((1, ts, d), lambda bi, t: (bi, t, 0)),
        out_shape=jax.ShapeDtypeStruct((b, s, d), F32),
        compiler_params=_params("parallel", "arbitrary"),
        name="pool",
    )(h, h, gain, w_grp, scale)


def _pick(n, pref):
    for c in range(min(pref, n), 0, -1):
        if n % c == 0 and (c % SUBLANES == 0 or c == n):
            return c
    return n


def kernel(x, ffn1_norm, ffn1_wg, ffn1_wu, ffn1_wd, mix_norm, ffn2_norm, ffn2_wg, ffn2_wu, ffn2_wd,
           ab_w_in, ab_w_out, rwkv_mu, rwkv_w0, rwkv_w2, rwkv_a0, rwkv_a2, rwkv_g2, rwkv_k_k, rwkv_k_a,
           rwkv_r_k, rwkv_gn_g, rwkv_gn_b, pool_w, pool_scale, final_norm):
    b, s, d = x.shape
    t = b * s
    depth = ffn1_norm.shape[0]
    bw = rwkv_w0.shape[-1]
    aw = ab_w_out.shape[1] - bw
    a_heads = aw // A_HEAD_DIM
    zb_col = 3 * aw
    assert rwkv_w2.shape[1] == LORA and 3 * LORA <= 2 * LANES

    w1g, w1u, w1d = ffn1_wg.astype(BF16), ffn1_wu.astype(BF16), ffn1_wd.astype(BF16)
    w2g, w2u, w2d = ffn2_wg.astype(BF16), ffn2_wu.astype(BF16), ffn2_wd.astype(BF16)
    n_in = ab_w_in.shape[-1]
    n_pad = zb_col + 3 * bw + 2 * LANES - n_in
    w_in = jnp.pad(ab_w_in, ((0, 0), (0, 0), (0, n_pad))).astype(BF16)
    w_out = ab_w_out.astype(BF16)
    pool_wb = pool_w.astype(BF16)
    zrow = jnp.zeros((LANES - LORA, bw), F32)

    tm = _pick(t, 1024)
    tmo = _pick(t, 512)
    tf = _pick(w1g.shape[-1], 512)
    row = lambda v: v.reshape(1, -1)

    h = x.reshape(t, d)
    for layer in range(depth):
        h = _ffn(h, row(ffn1_norm[layer]), w1g, w1u, w1d, layer, row(final_norm),
                 final=False, tm=tm, tf=tf)
        if layer % 2 == 0:
            e = layer // 2
            z = _inproj(h, row(mix_norm[layer]), w_in, e, tm=tm, tn=_pick(w_in.shape[-1], 1280))
            z = z.reshape(b, s, -1)
            ya = _moba(z, heads=a_heads, hg=math.gcd(a_heads, MOBA_HEAD_GROUP))
            mu = rwkv_mu[e]
            prm = {
                "mu_r": row(mu[0:bw]), "mu_k": row(mu[bw:2 * bw]), "mu_v": row(mu[2 * bw:3 * bw]),
                "mu_l": row(jnp.pad(mu[3 * bw:], (0, n_pad))),
                "w0": row(rwkv_w0[e]), "a0": row(rwkv_a0[e]), "k_k": row(rwkv_k_k[e]),
                "k_a": row(rwkv_k_a[e]), "r_k": row(rwkv_r_k[e]),
                "w2": jnp.concatenate([rwkv_w2[e], zrow], axis=0).astype(BF16),
                "a2": jnp.concatenate([zrow, rwkv_a2[e]], axis=0).astype(BF16),
                "g2": jnp.concatenate([rwkv_g2[e], zrow], axis=0).astype(BF16),
            }
            prep = _rwkv_prep(z, prm, zb_col=zb_col, ts=_pick(s, 256))
            yb = _rwkv_scan(prep, row(rwkv_gn_g[e]), row(rwkv_gn_b[e]), ts=_pick(s, 256))
            h = _outproj(h, ya.reshape(t, aw), yb.reshape(t, bw), w_out, e, tm=tmo, tn=d)
        else:
            o = layer // 2
            h = _pool(h.reshape(b, s, d), row(mix_norm[layer]), pool_wb, o, row(pool_scale[o]),
                      ts=_pick(s, 512)).reshape(t, d)
        h = _ffn(h, row(ffn2_norm[layer]), w2g, w2u, w2d, layer, row(final_norm),
                 final=(layer == depth - 1), tm=tm, tf=tf)
    return h.reshape(b, s, d)
```

```python
import functools
import math

import jax
import jax.numpy as jnp
from jax import lax
from jax.experimental import pallas as pl
from jax.experimental.pallas import tpu as pltpu

F32 = jnp.float32
BF16 = jnp.bfloat16

A_HEAD_DIM = 128
MOBA_BLOCK = 256
MOBA_TOPK = 3
B_HEAD_DIM = 64
LORA = 64
GN_EPS = 64e-5
RMS_EPS = 1e-6
POOL_WINDOWS = (2, 4, 8, 16)

LANES = 128
SUBLANES = 8
VMEM_LIMIT_BYTES = 56 * 1024 * 1024
FFN_VMEM_LIMIT_BYTES = 58 * 1024 * 1024

LOG2E = math.log2(math.e)
NEG_BIG = -1e30
MOBA_HEAD_GROUP = 4
RWKV_UNROLL = 2
RWKV_CHUNK = 64

TOKEN_TILE = 1024
FF_TILE = 512
INPROJ_COL_TILE = 1280
OUTPROJ_TOKEN_TILE = 512
RWKV_PREP_TILE = 256
RWKV_SCAN_TILE = 512
POOL_TILE = 512

_NT = (((1,), (1,)), ((), ()))


def _rms(x, gain):
    ms = jnp.mean(x * x, axis=-1, keepdims=True)
    return x * lax.rsqrt(ms + RMS_EPS) * gain


def _params(*sem, vmem=VMEM_LIMIT_BYTES):
    return pltpu.CompilerParams(dimension_semantics=sem, vmem_limit_bytes=vmem)


def _ffn_body(*refs, final, cast_next):
    h_ref, g_ref, wg_ref, wu_ref, wd_ref, fg_ref = refs[:6]
    if cast_next:
        ng_ref, nu_ref, nd_ref, o_ref, cg_ref, cu_ref, cd_ref, n_ref = refs[6:]
        cg_ref[...] = ng_ref[...].astype(BF16)
        cu_ref[...] = nu_ref[...].astype(BF16)
        cd_ref[...] = nd_ref[...].astype(BF16)
    else:
        o_ref, n_ref = refs[6:]
    f = pl.program_id(1)

    @pl.when(f == 0)
    def _():
        h = h_ref[...]
        n_ref[...] = _rms(h, g_ref[...]).astype(BF16)
        o_ref[...] = h

    n = n_ref[...]
    gate = jnp.dot(n, wg_ref[...], preferred_element_type=F32)
    up = jnp.dot(n, wu_ref[...], preferred_element_type=F32)
    act = (gate * jax.nn.sigmoid(gate) * up).astype(BF16)
    o_ref[...] += 0.5 * jnp.dot(act, wd_ref[...], preferred_element_type=F32)

    if final:
        @pl.when(f == pl.num_programs(1) - 1)
        def _():
            o_ref[...] = _rms(o_ref[...], fg_ref[...])


def _ffn(h, gain, wg, wu, wd, final_gain, nxt, *, final, tm, tf):
    t, d = h.shape
    f = wg.shape[-1]
    ni, nj = t // tm, f // tf
    in_specs = [
        pl.BlockSpec((tm, d), lambda i, j: (i, 0)),
        pl.BlockSpec((1, d), lambda i, j: (0, 0)),
        pl.BlockSpec((d, tf), lambda i, j: (0, j)),
        pl.BlockSpec((d, tf), lambda i, j: (0, j)),
        pl.BlockSpec((tf, d), lambda i, j: (j, 0)),
        pl.BlockSpec((1, d), lambda i, j: (0, 0)),
    ]
    out_specs = [pl.BlockSpec((tm, d), lambda i, j: (i, 0))]
    out_shape = [jax.ShapeDtypeStruct((t, d), F32)]
    args = [h, gain, wg, wu, wd, final_gain]
    if nxt is not None:
        ng, nu, nd, layer = nxt
        dr = d // ni
        assert d % ni == 0 and dr % LANES == 0 and ng.shape[1:] == (d, f) and nd.shape[1:] == (f, d)
        in_specs += [
            pl.BlockSpec((None, dr, tf), lambda i, j: (layer, i, j)),
            pl.BlockSpec((None, dr, tf), lambda i, j: (layer, i, j)),
            pl.BlockSpec((None, tf, dr), lambda i, j: (layer, j, i)),
        ]
        out_specs += [
            pl.BlockSpec((dr, tf), lambda i, j: (i, j)),
            pl.BlockSpec((dr, tf), lambda i, j: (i, j)),
            pl.BlockSpec((tf, dr), lambda i, j: (j, i)),
        ]
        out_shape += [jax.ShapeDtypeStruct((d, f), BF16), jax.ShapeDtypeStruct((d, f), BF16),
                      jax.ShapeDtypeStruct((f, d), BF16)]
        args += [ng, nu, nd]
    outs = pl.pallas_call(
        functools.partial(_ffn_body, final=final, cast_next=nxt is not None),
        grid=(ni, nj),
        in_specs=in_specs,
        out_specs=out_specs,
        out_shape=out_shape,
        scratch_shapes=[pltpu.VMEM((tm, d), BF16)],
        compiler_params=_params("parallel", "arbitrary", vmem=FFN_VMEM_LIMIT_BYTES),
        name="ffn",
    )(*args)
    return outs[0] if nxt is None else (outs[0], tuple(outs[1:]))


def _inproj_body(h_ref, g_ref, w_ref, z_ref, n_ref):
    @pl.when(pl.program_id(1) == 0)
    def _():
        n_ref[...] = _rms(h_ref[...], g_ref[...]).astype(BF16)

    z_ref[...] = jnp.dot(n_ref[...], w_ref[...], preferred_element_type=F32)


def _inproj(h, gain, w_in, e, *, tm, tn):
    t, d = h.shape
    n = w_in.shape[-1]
    return pl.pallas_call(
        _inproj_body,
        grid=(t // tm, n // tn),
        in_specs=[
            pl.BlockSpec((tm, d), lambda i, j: (i, 0)),
            pl.BlockSpec((1, d), lambda i, j: (0, 0)),
            pl.BlockSpec((None, d, tn), lambda i, j: (e, 0, j)),
        ],
        out_specs=pl.BlockSpec((tm, tn), lambda i, j: (i, j)),
        out_shape=jax.ShapeDtypeStruct((t, n), F32),
        scratch_shapes=[pltpu.VMEM((tm, d), BF16)],
        compiler_params=_params("parallel", "arbitrary"),
        name="inproj",
    )(h, gain, w_in)


def _split_bf16(x, parts):
    out = []
    for _ in range(parts):
        p = x.astype(BF16)
        out.append(p)
        x = x - p.astype(F32)
    return out


def _nt_dot_3pass(a, b):
    ah, al = _split_bf16(a, 2)
    bh, bl = _split_bf16(b, 2)
    n = a.shape[0]
    both = lax.dot_general(jnp.concatenate([ah, al], axis=0), bh, _NT, preferred_element_type=F32)
    return both[0:n] + both[n:] + lax.dot_general(ah, bl, _NT, preferred_element_type=F32)


def _moba_body(q_ref, k_ref, v_ref, o_ref, kb_ref, vt_ref, km_ref, sel_ref, s0_ref, s1_ref, acc_ref,
               *, hg, nb, blk, topk, scale):
    i = pl.program_id(2)
    dh = A_HEAD_DIM
    cols = [slice(hh * dh, (hh + 1) * dh) for hh in range(hg)]

    @pl.when(i == 0)
    def _():
        km_ref[...] = jnp.zeros_like(km_ref)
        for hh in range(hg):
            for jb in range(nb):
                rows = slice(jb * blk, (jb + 1) * blk)
                kf = k_ref[0, rows, cols[hh]]
                kb_ref[hh, jb] = kf.astype(BF16)
                vt_ref[hh, jb] = v_ref[0, rows, cols[hh]].T.astype(BF16)
                km_ref[hh, jb:jb + 1, :] = jnp.mean(kf, axis=0, keepdims=True)

    qs = [q_ref[0, :, c] for c in cols]
    for hh in range(hg):
        gate = _nt_dot_3pass(km_ref[hh], qs[hh])
        blk_id = lax.broadcasted_iota(jnp.int32, gate.shape, 0)
        rank = jnp.zeros(gate.shape, jnp.int32)
        for jp in range(nb):
            row = gate[jp:jp + 1, :]
            beats = (row > gate) | ((row == gate) & (jp < blk_id))
            rank = rank + jnp.where(beats & (jp < i), 1, 0)
        sel_ref[hh] = jnp.where((blk_id < i) & (rank < topk), 0.0, NEG_BIG)

    qb = [(q * (scale * LOG2E)).astype(BF16) for q in qs]

    def scores(hh, j):
        return lax.dot_general(kb_ref[hh, j], qb[hh], _NT, preferred_element_type=F32)

    def pv(hh, p, j):
        return jnp.dot(vt_ref[hh, j], p.astype(BF16), preferred_element_type=F32)

    kpos = lax.broadcasted_iota(jnp.int32, (blk, blk), 0)
    qpos = lax.broadcasted_iota(jnp.int32, (blk, blk), 1)
    causal = kpos <= qpos
    heads = range(hg)
    ss = [jnp.where(causal, scores(hh, i), NEG_BIG) for hh in heads]
    ms = [jnp.max(s, axis=0, keepdims=True) for s in ss]
    ps = [jnp.exp2(s - m) for s, m in zip(ss, ms)]
    ls = [jnp.sum(p, axis=0, keepdims=True) for p in ps]
    accs = [pv(hh, ps[hh], i) for hh in heads]

    for hh in heads:
        acc_ref[hh] = accs[hh]

    def produce(j, s_ref):
        mx = []
        for hh in heads:
            s = scores(hh, jnp.minimum(j, nb - 1))
            s_ref[hh] = s
            mx.append(jnp.max(s, axis=0, keepdims=True))
        return tuple(mx)

    def consume(j, s_ref, ms, ls, mx):
        sel = [sel_ref[hh, pl.ds(j, 1), :] for hh in heads]
        m_new = [jnp.maximum(m, x + b) for m, x, b in zip(ms, mx, sel)]
        alpha = [jnp.exp2(m - mn) for m, mn in zip(ms, m_new)]
        shift = [mn - 2.0 * b for mn, b in zip(m_new, sel)]
        ps = [jnp.exp2(s_ref[hh] - shift[hh]) for hh in heads]
        ls = [a * l + jnp.sum(p, axis=0, keepdims=True) for a, l, p in zip(alpha, ls, ps)]
        pvs = [pv(hh, ps[hh], j) for hh in heads]
        for hh in heads:
            acc_ref[hh] = alpha[hh] * acc_ref[hh] + pvs[hh]
        return tuple(m_new), tuple(ls)

    def past(jj, carry):
        ms, ls, mx0 = carry
        j0 = 2 * jj
        mx1 = produce(j0 + 1, s1_ref)
        ms, ls = consume(j0, s0_ref, ms, ls, mx0)
        mx0 = produce(j0 + 2, s0_ref)
        ms, ls = consume(j0 + 1, s1_ref, ms, ls, mx1)
        return ms, ls, mx0

    mx0 = produce(0, s0_ref)
    _, ls, _ = lax.fori_loop(0, (i + 1) // 2, past, (tuple(ms), tuple(ls), mx0))
    for hh in heads:
        o_ref[0, :, cols[hh]] = (acc_ref[hh] / ls[hh]).T.astype(o_ref.dtype)


def _moba(z, *, heads, hg):
    b, s, _ = z.shape
    dh, blk = A_HEAD_DIM, MOBA_BLOCK
    nb = s // blk
    nbp = -(-nb // SUBLANES) * SUBLANES
    ng = heads // hg
    assert s % blk == 0 and heads % hg == 0
    return pl.pallas_call(
        functools.partial(_moba_body, hg=hg, nb=nb, blk=blk, topk=min(MOBA_TOPK, nb), scale=dh ** -0.5),
        grid=(b, ng, nb),
        in_specs=[
            pl.BlockSpec((1, blk, hg * dh), lambda bi, g, i: (bi, i, g)),
            pl.BlockSpec((1, s, hg * dh), lambda bi, g, i: (bi, 0, ng + g)),
            pl.BlockSpec((1, s, hg * dh), lambda bi, g, i: (bi, 0, 2 * ng + g)),
        ],
        out_specs=pl.BlockSpec((1, blk, hg * dh), lambda bi, g, i: (bi, i, g)),
        out_shape=jax.ShapeDtypeStruct((b, s, heads * dh), BF16),
        scratch_shapes=[pltpu.VMEM((hg, nb, blk, dh), BF16), pltpu.VMEM((hg, nb, dh, blk), BF16),
                        pltpu.VMEM((hg, nbp, dh), F32), pltpu.VMEM((hg, nbp, blk), F32),
                        pltpu.VMEM((hg, blk, blk), F32), pltpu.VMEM((hg, blk, blk), F32),
                        pltpu.VMEM((hg, dh, blk), F32)],
        compiler_params=_params("parallel", "parallel", "arbitrary"),
        name="moba",
    )(z, z, z)


def _split_dot(x, ones_bd):
    hi = x.astype(BF16)
    lo = (x - hi.astype(F32)).astype(BF16)
    return (jnp.dot(hi, ones_bd, preferred_element_type=F32)
            + jnp.dot(lo, ones_bd, preferred_element_type=F32))


def _head_sum(x, ones_bd):
    cols = [_split_dot(x[:, c:c + LANES], ones_bd) for c in range(0, x.shape[-1], LANES)]
    return cols[0] if len(cols) == 1 else jnp.concatenate(cols, axis=-1)


def _head_ones():
    r = lax.broadcasted_iota(jnp.int32, (LANES, LANES), 0) // B_HEAD_DIM
    c = lax.broadcasted_iota(jnp.int32, (LANES, LANES), 1) // B_HEAD_DIM
    return jnp.where(r == c, 1.0, 0.0).astype(BF16)


def _prep_body(zr_ref, zk_ref, zv_ref, zl_ref, pr_ref, pk_ref, pv_ref, plo_ref,
               mur_ref, muk_ref, muv_ref, mul_ref, w0_ref, a0_ref, kk_ref, ka_ref, rk_ref,
               w2_ref, a2_ref, g2_ref, tri_ref, ones_ref,
               at_out, rp_out, bm_out, km_out, bc_out, kc_out, v_out, pc_out, bonus_out, g_out):
    ts = zr_ref.shape[1]
    first = pl.program_id(1) == 0
    row8 = lax.broadcasted_iota(jnp.int32, (SUBLANES, 1), 0)

    def lerp(z_ref, prev_ref, mu_ref):
        x = z_ref[0]
        prev = jnp.where(first, 0.0, prev_ref[0, SUBLANES - 1:SUBLANES, :])
        rolled = pltpu.roll(x, 1, axis=0)
        head = jnp.where(row8 == 0, prev, rolled[0:SUBLANES])
        shifted = jnp.concatenate([head, rolled[SUBLANES:]], axis=0)
        return x + (shifted - x) * mu_ref[...]

    r = lerp(zr_ref, pr_ref, mur_ref)
    k = lerp(zk_ref, pk_ref, muk_ref)
    v = lerp(zv_ref, pv_ref, muv_ref)
    lo = lerp(zl_ref, plo_ref, mul_ref)
    wa = lo[:, 0:LANES]
    gl = lo[:, LANES:2 * LANES]

    w_in = w0_ref[...] + jnp.dot(jnp.tanh(wa).astype(BF16), w2_ref[...], preferred_element_type=F32)
    lw = -math.exp(-0.5) * jax.nn.sigmoid(w_in)
    a = jax.nn.sigmoid(a0_ref[...] + jnp.dot(wa.astype(BF16), a2_ref[...], preferred_element_type=F32))
    g = jnp.dot(jax.nn.sigmoid(gl).astype(BF16), g2_ref[...], preferred_element_type=F32)

    ones_bd = ones_ref[...]
    kk = k * kk_ref[...]
    kk = kk * lax.rsqrt(jnp.maximum(_head_sum(kk * kk, ones_bd), 1e-24))
    k2 = k * (1.0 + (a - 1.0) * ka_ref[...])
    bonus = _head_sum(r * k2 * rk_ref[...], ones_bd) * v

    c_len = RWKV_CHUNK
    nc = ts // c_len
    bw = lw.shape[1]
    lw2 = lw * LOG2E
    y = jnp.dot(tri_ref[...], jnp.concatenate(_split_bf16(lw2, 3), axis=1), preferred_element_type=F32)
    cum = y[:, 0:bw] + y[:, bw:2 * bw] + y[:, 2 * bw:]
    last = cum.reshape(nc, c_len, bw)[:, c_len - 1:c_len, :]
    total = jnp.broadcast_to(last, (nc, c_len, bw)).reshape(ts, bw)
    p_inv = jnp.exp2(-cum)
    to_end = jnp.exp2(total - cum)
    b = kk * a
    at_out[0] = (-kk * jnp.exp2(cum - lw2)).astype(BF16)
    rp_out[0] = (r * jnp.exp2(cum)).astype(BF16)
    bm_out[0] = (b * p_inv).astype(BF16)
    km_out[0] = (k2 * p_inv).astype(BF16)
    bc_out[0] = (b * to_end).astype(BF16)
    kc_out[0] = (k2 * to_end).astype(BF16)
    v_out[0] = v.astype(BF16)
    pc_out[0] = jnp.exp2(jnp.broadcast_to(last, (nc, SUBLANES, bw)))
    bonus_out[0] = bonus
    g_out[0] = g


def _rwkv_prep(z, p, *, zb_col, ts):
    b, s, _ = z.shape
    bw = p["w0"].shape[-1]
    lw = 2 * LANES
    cb = zb_col // bw
    lb = (zb_col + 3 * bw) // lw
    assert zb_col % bw == 0 and (zb_col + 3 * bw) % lw == 0 and s % ts == 0 and ts % SUBLANES == 0
    sub = ts // SUBLANES

    def cur(col, width):
        return pl.BlockSpec((1, ts, width), lambda bi, t: (bi, t, col))

    def prev(col, width):
        return pl.BlockSpec((1, SUBLANES, width), lambda bi, t: (bi, jnp.maximum(t * sub - 1, 0), col))

    def vec(width):
        return pl.BlockSpec((1, width), lambda bi, t: (0, 0))

    def mat(width):
        return pl.BlockSpec((LANES, width), lambda bi, t: (0, 0))

    assert ts % RWKV_CHUNK == 0
    nc = ts // RWKV_CHUNK
    ti = jnp.arange(ts)
    tri = ((ti[:, None] // RWKV_CHUNK == ti[None, :] // RWKV_CHUNK) & (ti[None, :] <= ti[:, None])).astype(BF16)
    li = jnp.arange(LANES)
    ones_bd = (li[:, None] // B_HEAD_DIM == li[None, :] // B_HEAD_DIM).astype(BF16)

    def whole(shape):
        return pl.BlockSpec(shape, lambda bi, t: (0, 0))

    out_spec = pl.BlockSpec((1, ts, bw), lambda bi, t: (bi, t, 0))
    pc_spec = pl.BlockSpec((1, nc, SUBLANES, bw), lambda bi, t: (bi, t, 0, 0))
    tok16 = jax.ShapeDtypeStruct((b, s, bw), BF16)
    tok32 = jax.ShapeDtypeStruct((b, s, bw), F32)
    pc_shape = jax.ShapeDtypeStruct((b, s // RWKV_CHUNK, SUBLANES, bw), F32)
    return pl.pallas_call(
        _prep_body,
        grid=(b, s // ts),
        in_specs=[cur(cb, bw), cur(cb + 1, bw), cur(cb + 2, bw), cur(lb, lw),
                  prev(cb, bw), prev(cb + 1, bw), prev(cb + 2, bw), prev(lb, lw),
                  vec(bw), vec(bw), vec(bw), vec(lw), vec(bw), vec(bw), vec(bw), vec(bw), vec(bw),
                  mat(bw), mat(bw), mat(bw),
                  whole((ts, ts)), whole((LANES, LANES))],
        out_specs=[out_spec] * 7 + [pc_spec, out_spec, out_spec],
        out_shape=[tok16] * 7 + [pc_shape, tok32, tok32],
        compiler_params=_params("parallel", "arbitrary"),
        name="rwkv_prep",
    )(z, z, z, z, z, z, z, z,
      p["mu_r"], p["mu_k"], p["mu_v"], p["mu_l"], p["w0"], p["a0"], p["k_k"], p["k_a"], p["r_k"],
      p["w2"], p["a2"], p["g2"], tri, ones_bd)


def _bdot(a, b):
    return jnp.dot(a.astype(BF16), b.astype(BF16), preferred_element_type=F32)


def _scan_body(at_ref, rp_ref, bm_ref, km_ref, bc_ref, kc_ref, v_ref, pc_ref, bonus_ref, g_ref, gng_ref, gnb_ref,
               o_ref, h_ref, *, chunks, pairs, unroll):
    c_len = RWKV_CHUNK
    hd = B_HEAD_DIM
    prs = range(pairs)
    lanes = [slice(p * LANES, (p + 1) * LANES) for p in prs]

    @pl.when(pl.program_id(1) == 0)
    def _():
        h_ref[...] = jnp.zeros_like(h_ref)

    lane = lax.broadcasted_iota(jnp.int32, (c_len, LANES), 1)
    rowc = lax.broadcasted_iota(jnp.int32, (c_len, LANES), 0)
    head0 = lane < hd
    scol = lane % hd
    strict = scol < rowc
    incl = scol <= rowc
    r2 = lax.broadcasted_iota(jnp.int32, (LANES, LANES), 0)
    c2 = lax.broadcasted_iota(jnp.int32, (LANES, LANES), 1)
    bd_mask = (r2 // hd) == (c2 // hd)
    eye_cat = jnp.where(scol == rowc, 1.0, 0.0)
    ones_bd = _head_ones()
    gng = [gng_ref[:, ln] for ln in lanes]
    gnb = [gnb_ref[:, ln] for ln in lanes]

    def stack(x):
        return jnp.concatenate([jnp.where(head0, x, 0.0), jnp.where(head0, 0.0, x)], axis=0)

    def chunk_step(c, carry):
        rows = [pl.ds(pl.multiple_of((c * unroll + ci) * c_len, c_len), c_len) for ci in range(unroll)]
        units = [(ci, p) for ci in range(unroll) for p in prs]
        us = range(len(units))

        def load(ref):
            return [ref[0, rows[ci], lanes[p]] for ci, p in units]

        at, rp, bm, km, v = load(at_ref), load(rp_ref), load(bm_ref), load(km_ref), load(v_ref)
        bc_t = [x.astype(F32).T for x in load(bc_ref)]
        kc_t = [x.astype(F32).T for x in load(kc_ref)]
        pc_col = [pc_ref[0, c * unroll + ci, :, lanes[p]].T[:, 0:1] for ci, p in units]

        lhs = [jnp.concatenate([at[u], rp[u]], axis=0) for u in us]
        rhs = [jnp.concatenate([stack(bm[u]), stack(km[u])], axis=0) for u in us]
        amat = [lax.dot_general(lhs[u], rhs[u], _NT, preferred_element_type=F32) for u in us]
        a_ab = [jnp.where(strict, m[0:c_len, 0:LANES], 0.0) for m in amat]
        a_ak = [jnp.where(strict, m[0:c_len, LANES:2 * LANES], 0.0) for m in amat]
        a_rb = [jnp.where(incl, m[c_len:, 0:LANES], 0.0) for m in amat]
        a_rk = [jnp.where(incl, m[c_len:, LANES:2 * LANES], 0.0) for m in amat]

        pw = a_ab
        t_inv = [eye_cat + x for x in a_ab]
        pw_bd = [stack(x) for x in pw]
        pw = [_bdot(pw[u], pw_bd[u]) for u in us]
        for _ in range(int(math.log2(c_len)) - 2):
            pw_bd = [stack(x) for x in pw]
            both = [_bdot(jnp.concatenate([pw[u], t_inv[u]], axis=0), pw_bd[u]) for u in us]
            pw = [x[0:c_len] for x in both]
            t_inv = [t_inv[u] + both[u][c_len:] for u in us]
        pw_bd = [stack(x) for x in pw]
        t_inv = [t_inv[u] + _bdot(t_inv[u], pw_bd[u]) for u in us]

        v_st = [stack(x) for x in v]
        akv = [_bdot(a_ak[u], v_st[u]) for u in us]
        sol = [_bdot(t_inv[u], jnp.concatenate([stack(at[u]), stack(akv[u])], axis=1)) for u in us]
        y0 = [_bdot(a_rk[u], v_st[u]) for u in us]
        hkv = [jnp.where(bd_mask, _bdot(kc_t[u], v[u]), 0.0) for u in us]

        h = [h_ref[p] for p in prs]
        ys = []
        for ci in range(unroll):
            un = [ci * pairs + p for p in prs]
            hb = [x.astype(BF16) for x in h]
            uu = [jnp.dot(sol[un[p]][:, 0:LANES].astype(BF16), hb[p], preferred_element_type=F32)
                  + sol[un[p]][:, LANES:] for p in prs]
            y = [jnp.dot(rp[un[p]].astype(BF16), hb[p], preferred_element_type=F32) + y0[un[p]] for p in prs]
            ys += [y[p] + _bdot(a_rb[un[p]], stack(uu[p])) for p in prs]
            h = [pc_col[un[p]] * h[p] + jnp.where(bd_mask, _bdot(bc_t[un[p]], uu[p]), 0.0) + hkv[un[p]]
                 for p in prs]
        for p in prs:
            h_ref[p] = h[p]

        mean = [_split_dot(x, ones_bd) * (1.0 / hd) for x in ys]
        dlt = [ys[u] - mean[u] for u in us]
        var = [_split_dot(x * x, ones_bd) * (1.0 / hd) for x in dlt]
        for u, (ci, p) in enumerate(units):
            yn = dlt[u] * lax.rsqrt(var[u] + GN_EPS) * gng[p] + gnb[p]
            out = (yn + bonus_ref[0, rows[ci], lanes[p]]) * g_ref[0, rows[ci], lanes[p]]
            o_ref[0, rows[ci], lanes[p]] = out.astype(o_ref.dtype)
        return carry

    lax.fori_loop(0, chunks // unroll, chunk_step, 0)


def _rwkv_scan(prep, gn_g, gn_b, *, ts):
    b, s, bw = prep[0].shape
    assert ts % (RWKV_CHUNK * RWKV_UNROLL) == 0 and s % ts == 0 and bw % LANES == 0
    pairs = bw // LANES
    tok = pl.BlockSpec((1, ts, bw), lambda bi, t: (bi, t, 0))
    pcs = pl.BlockSpec((1, ts // RWKV_CHUNK, SUBLANES, bw), lambda bi, t: (bi, t, 0, 0))
    vec = pl.BlockSpec((1, bw), lambda bi, t: (0, 0))
    return pl.pallas_call(
        functools.partial(_scan_body, chunks=ts // RWKV_CHUNK, pairs=pairs, unroll=RWKV_UNROLL),
        grid=(b, s // ts),
        in_specs=[tok] * 7 + [pcs, tok, tok, vec, vec],
        out_specs=tok,
        out_shape=jax.ShapeDtypeStruct((b, s, bw), BF16),
        scratch_shapes=[pltpu.VMEM((pairs, LANES, LANES), F32)],
        compiler_params=_params("parallel", "arbitrary"),
        name="rwkv_scan",
    )(*prep, gn_g, gn_b)


def _outproj_body(h_ref, ya_ref, yb_ref, wa_ref, wb_ref, o_ref):
    acc = jnp.dot(ya_ref[...].astype(BF16), wa_ref[...], preferred_element_type=F32)
    acc = acc + jnp.dot(yb_ref[...].astype(BF16), wb_ref[...], preferred_element_type=F32)
    o_ref[...] = h_ref[...] + acc


def _outproj(h, ya, yb, w_out, e, *, tm, tn):
    t, d = h.shape
    wa, wb = ya.shape[-1], yb.shape[-1]
    assert wa == wb
    return pl.pallas_call(
        _outproj_body,
        grid=(t // tm, d // tn),
        in_specs=[
            pl.BlockSpec((tm, tn), lambda i, j: (i, j)),
            pl.BlockSpec((tm, wa), lambda i, j: (i, 0)),
            pl.BlockSpec((tm, wb), lambda i, j: (i, 0)),
            pl.BlockSpec((None, wa, tn), lambda i, j: (e, 0, j)),
            pl.BlockSpec((None, wb, tn), lambda i, j: (e, 1, j)),
        ],
        out_specs=pl.BlockSpec((tm, tn), lambda i, j: (i, j)),
        out_shape=jax.ShapeDtypeStruct((t, d), F32),
        compiler_params=_params("parallel", "arbitrary"),
        name="outproj",
    )(h, ya, yb, w_out, w_out)


def _pool_body(h_ref, halo_ref, g_ref, w_ref, sc_ref, o_ref, *, windows, halo):
    ts, d = h_ref.shape[1], h_ref.shape[2]
    grp = d // len(windows)
    t = pl.program_id(1)
    hcur = h_ref[0]
    u = _rms(hcur, g_ref[...])
    uh = jnp.where(t == 0, 0.0, _rms(halo_ref[0], g_ref[...]))
    pos = t * ts + lax.broadcasted_iota(jnp.int32, (ts, 1), 0)
    outs = []
    for gi, win in enumerate(windows):
        cols = slice(gi * grp, (gi + 1) * grp)
        ug = u[:, cols]
        ext = jnp.concatenate([uh[:, cols], ug], axis=0)
        span = 1
        while span < win:
            ext = ext + pltpu.roll(ext, span, axis=0)
            span *= 2
        cnt = jnp.minimum(pos + 1, win).astype(F32)
        diff = ext[halo:, :] / cnt - ug
        outs.append(jnp.dot(diff.astype(BF16), w_ref[gi], preferred_element_type=F32))
    o_ref[0] = hcur + jnp.concatenate(outs, axis=-1) * sc_ref[...]


def _pool(h, gain, w_grp, o, scale, *, ts):
    b, s, d = h.shape
    halo = 2 * SUBLANES
    assert max(POOL_WINDOWS) <= halo and all(w & (w - 1) == 0 for w in POOL_WINDOWS)
    assert s % ts == 0 and ts % halo == 0
    ng, grp = w_grp.shape[1], w_grp.shape[2]
    sub = ts // halo
    return pl.pallas_call(
        functools.partial(_pool_body, windows=POOL_WINDOWS, halo=halo),
        grid=(b, s // ts),
        in_specs=[
            pl.BlockSpec((1, ts, d), lambda bi, t: (bi, t, 0)),
            pl.BlockSpec((1, halo, d), lambda bi, t: (bi, jnp.maximum(t * sub - 1, 0), 0)),
            pl.BlockSpec((1, d), lambda bi, t: (0, 0)),
            pl.BlockSpec((None, ng, grp, grp), lambda bi, t: (o, 0, 0, 0)),
            pl.BlockSpec((1, d), lambda bi, t: (0, 0)),
        ],
        out_specs=pl.BlockSpec((1, ts, d), lambda bi, t: (bi, t, 0)),
        out_shape=jax.ShapeDtypeStruct((b, s, d), F32),
        compiler_params=_params("parallel", "arbitrary"),
        name="pool",
    )(h, h, gain, w_grp, scale)


def _pick(n, pref):
    for c in range(min(pref, n), 0, -1):
        if n % c == 0 and (c % SUBLANES == 0 or c == n):
            return c
    return n


def kernel(x, ffn1_norm, ffn1_wg, ffn1_wu, ffn1_wd, mix_norm, ffn2_norm, ffn2_wg, ffn2_wu, ffn2_wd,
           ab_w_in, ab_w_out, rwkv_mu, rwkv_w0, rwkv_w2, rwkv_a0, rwkv_a2, rwkv_g2, rwkv_k_k, rwkv_k_a,
           rwkv_r_k, rwkv_gn_g, rwkv_gn_b, pool_w, pool_scale, final_norm):
    b, s, d = x.shape
    t = b * s
    depth = ffn1_norm.shape[0]
    bw = rwkv_w0.shape[-1]
    aw = ab_w_out.shape[1] - bw
    a_heads = aw // A_HEAD_DIM
    zb_col = 3 * aw
    assert rwkv_w2.shape[1] == LORA and 3 * LORA <= 2 * LANES

    ffn_w = [(ffn1_wg, ffn1_wu, ffn1_wd), (ffn2_wg, ffn2_wu, ffn2_wd)]
    cur_w = tuple(w[0].astype(BF16) for w in ffn_w[0])
    n_in = ab_w_in.shape[-1]
    n_pad = zb_col + 3 * bw + 2 * LANES - n_in
    w_in = jnp.pad(ab_w_in, ((0, 0), (0, 0), (0, n_pad))).astype(BF16)
    w_out = ab_w_out.astype(BF16)
    pool_wb = pool_w.astype(BF16)
    zrow = jnp.zeros((LANES - LORA, bw), F32)

    tm = _pick(t, TOKEN_TILE)
    tmo = _pick(t, OUTPROJ_TOKEN_TILE)
    tf = _pick(ffn1_wg.shape[-1], FF_TILE)
    row = lambda v: v.reshape(1, -1)

    h = x.reshape(t, d)
    for layer in range(depth):
        h, cur_w = _ffn(h, row(ffn1_norm[layer]), *cur_w, row(final_norm), (*ffn_w[1], layer),
                        final=False, tm=tm, tf=tf)
        if layer % 2 == 0:
            e = layer // 2
            z = _inproj(h, row(mix_norm[layer]), w_in, e, tm=tm, tn=_pick(w_in.shape[-1], INPROJ_COL_TILE))
            z = z.reshape(b, s, -1)
            ya = _moba(z, heads=a_heads, hg=math.gcd(a_heads, MOBA_HEAD_GROUP))
            mu = rwkv_mu[e]
            prm = {
                "mu_r": row(mu[0:bw]), "mu_k": row(mu[bw:2 * bw]), "mu_v": row(mu[2 * bw:3 * bw]),
                "mu_l": row(jnp.pad(mu[3 * bw:], (0, n_pad))),
                "w0": row(rwkv_w0[e]), "a0": row(rwkv_a0[e]), "k_k": row(rwkv_k_k[e]),
                "k_a": row(rwkv_k_a[e]), "r_k": row(rwkv_r_k[e]),
                "w2": jnp.concatenate([rwkv_w2[e], zrow], axis=0).astype(BF16),
                "a2": jnp.concatenate([zrow, rwkv_a2[e]], axis=0).astype(BF16),
                "g2": jnp.concatenate([rwkv_g2[e], zrow], axis=0).astype(BF16),
            }
            prep = _rwkv_prep(z, prm, zb_col=zb_col, ts=_pick(s, RWKV_PREP_TILE))
            yb = _rwkv_scan(prep, row(rwkv_gn_g[e]), row(rwkv_gn_b[e]), ts=_pick(s, RWKV_SCAN_TILE))
            h = _outproj(h, ya.reshape(t, aw), yb.reshape(t, bw), w_out, e, tm=tmo, tn=d)
        else:
            o = layer // 2
            h = _pool(h.reshape(b, s, d), row(mix_norm[layer]), pool_wb, o, row(pool_scale[o]),
                      ts=_pick(s, POOL_TILE)).reshape(t, d)
        if layer + 1 < depth:
            h, cur_w = _ffn(h, row(ffn2_norm[layer]), *cur_w, row(final_norm), (*ffn_w[0], layer + 1),
                            final=False, tm=tm, tf=tf)
        else:
            h = _ffn(h, row(ffn2_norm[layer]), *cur_w, row(final_norm), None, final=True, tm=tm, tf=tf)
    return h.reshape(b, s, d)
```

```python
import functools
import math

import jax
import jax.numpy as jnp
from jax import lax
from jax.experimental import pallas as pl
from jax.experimental.pallas import tpu as pltpu

F32 = jnp.float32
BF16 = jnp.bfloat16

A_HEAD_DIM = 128
MOBA_BLOCK = 256
MOBA_TOPK = 3
B_HEAD_DIM = 64
LORA = 64
GN_EPS = 64e-5
RMS_EPS = 1e-6
POOL_WINDOWS = (2, 4, 8, 16)

LANES = 128
SUBLANES = 8
VMEM_LIMIT_BYTES = 56 * 1024 * 1024
FFN_VMEM_LIMIT_BYTES = 58 * 1024 * 1024

LOG2E = math.log2(math.e)
NEG_BIG = -1e30
MOBA_HEAD_GROUP = 4
RWKV_UNROLL = 2
RWKV_CHUNK = 64

TOKEN_TILE = 1024
FF_TILE = 512
INPROJ_COL_TILE = 1280
OUTPROJ_TOKEN_TILE = 512
RWKV_PREP_TILE = 512
RWKV_SCAN_TILE = 512
POOL_TILE = 512

_NT = (((1,), (1,)), ((), ()))


def _rms(x, gain):
    ms = jnp.mean(x * x, axis=-1, keepdims=True)
    return x * lax.rsqrt(ms + RMS_EPS) * gain


def _params(*sem, vmem=VMEM_LIMIT_BYTES):
    return pltpu.CompilerParams(dimension_semantics=sem, vmem_limit_bytes=vmem)


def _ffn_body(*refs, final, cast_next):
    h_ref, g_ref, wg_ref, wu_ref, wd_ref, fg_ref = refs[:6]
    if cast_next:
        ng_ref, nu_ref, nd_ref, o_ref, cg_ref, cu_ref, cd_ref, n_ref = refs[6:]
        cg_ref[...] = ng_ref[...].astype(BF16)
        cu_ref[...] = nu_ref[...].astype(BF16)
        cd_ref[...] = nd_ref[...].astype(BF16)
    else:
        o_ref, n_ref = refs[6:]
    f = pl.program_id(1)

    @pl.when(f == 0)
    def _():
        h = h_ref[...]
        n_ref[...] = _rms(h, g_ref[...]).astype(BF16)
        o_ref[...] = h

    n = n_ref[...]
    gate = jnp.dot(n, wg_ref[...], preferred_element_type=F32)
    up = jnp.dot(n, wu_ref[...], preferred_element_type=F32)
    act = (gate * jax.nn.sigmoid(gate) * up).astype(BF16)
    o_ref[...] += 0.5 * jnp.dot(act, wd_ref[...], preferred_element_type=F32)

    if final:
        @pl.when(f == pl.num_programs(1) - 1)
        def _():
            o_ref[...] = _rms(o_ref[...], fg_ref[...])


def _ffn(h, gain, wg, wu, wd, final_gain, nxt, *, final, tm, tf):
    t, d = h.shape
    f = wg.shape[-1]
    ni, nj = t // tm, f // tf
    in_specs = [
        pl.BlockSpec((tm, d), lambda i, j: (i, 0)),
        pl.BlockSpec((1, d), lambda i, j: (0, 0)),
        pl.BlockSpec((d, tf), lambda i, j: (0, j)),
        pl.BlockSpec((d, tf), lambda i, j: (0, j)),
        pl.BlockSpec((tf, d), lambda i, j: (j, 0)),
        pl.BlockSpec((1, d), lambda i, j: (0, 0)),
    ]
    out_specs = [pl.BlockSpec((tm, d), lambda i, j: (i, 0))]
    out_shape = [jax.ShapeDtypeStruct((t, d), F32)]
    args = [h, gain, wg, wu, wd, final_gain]
    if nxt is not None:
        ng, nu, nd, layer = nxt
        dr = d // ni
        assert d % ni == 0 and dr % LANES == 0 and ng.shape[1:] == (d, f) and nd.shape[1:] == (f, d)
        in_specs += [
            pl.BlockSpec((None, dr, tf), lambda i, j: (layer, i, j)),
            pl.BlockSpec((None, dr, tf), lambda i, j: (layer, i, j)),
            pl.BlockSpec((None, tf, dr), lambda i, j: (layer, j, i)),
        ]
        out_specs += [
            pl.BlockSpec((dr, tf), lambda i, j: (i, j)),
            pl.BlockSpec((dr, tf), lambda i, j: (i, j)),
            pl.BlockSpec((tf, dr), lambda i, j: (j, i)),
        ]
        out_shape += [jax.ShapeDtypeStruct((d, f), BF16), jax.ShapeDtypeStruct((d, f), BF16),
                      jax.ShapeDtypeStruct((f, d), BF16)]
        args += [ng, nu, nd]
    outs = pl.pallas_call(
        functools.partial(_ffn_body, final=final, cast_next=nxt is not None),
        grid=(ni, nj),
        in_specs=in_specs,
        out_specs=out_specs,
        out_shape=out_shape,
        scratch_shapes=[pltpu.VMEM((tm, d), BF16)],
        compiler_params=_params("parallel", "arbitrary", vmem=FFN_VMEM_LIMIT_BYTES),
        name="ffn",
    )(*args)
    return outs[0] if nxt is None else (outs[0], tuple(outs[1:]))


def _inproj_body(h_ref, g_ref, w_ref, z_ref, n_ref):
    @pl.when(pl.program_id(1) == 0)
    def _():
        n_ref[...] = _rms(h_ref[...], g_ref[...]).astype(BF16)

    z_ref[...] = jnp.dot(n_ref[...], w_ref[...], preferred_element_type=F32)


def _inproj(h, gain, w_in, e, *, tm, tn):
    t, d = h.shape
    n = w_in.shape[-1]
    return pl.pallas_call(
        _inproj_body,
        grid=(t // tm, n // tn),
        in_specs=[
            pl.BlockSpec((tm, d), lambda i, j: (i, 0)),
            pl.BlockSpec((1, d), lambda i, j: (0, 0)),
            pl.BlockSpec((None, d, tn), lambda i, j: (e, 0, j)),
        ],
        out_specs=pl.BlockSpec((tm, tn), lambda i, j: (i, j)),
        out_shape=jax.ShapeDtypeStruct((t, n), F32),
        scratch_shapes=[pltpu.VMEM((tm, d), BF16)],
        compiler_params=_params("parallel", "arbitrary"),
        name="inproj",
    )(h, gain, w_in)


def _split_bf16(x, parts):
    out = []
    for _ in range(parts):
        p = x.astype(BF16)
        out.append(p)
        x = x - p.astype(F32)
    return out


def _nt_dot_3pass(a, b):
    ah, al = _split_bf16(a, 2)
    bh, bl = _split_bf16(b, 2)
    n = a.shape[0]
    both = lax.dot_general(jnp.concatenate([ah, al], axis=0), bh, _NT, preferred_element_type=F32)
    return both[0:n] + both[n:] + lax.dot_general(ah, bl, _NT, preferred_element_type=F32)


def _moba_body(q_ref, k_ref, v_ref, o_ref, kb_ref, vt_ref, km_ref, sel_ref, s0_ref, s1_ref, acc_ref,
               *, hg, nb, blk, topk, scale):
    i = pl.program_id(2)
    dh = A_HEAD_DIM
    cols = [slice(hh * dh, (hh + 1) * dh) for hh in range(hg)]

    @pl.when(i == 0)
    def _():
        km_ref[...] = jnp.zeros_like(km_ref)
        for hh in range(hg):
            for jb in range(nb):
                rows = slice(jb * blk, (jb + 1) * blk)
                kf = k_ref[0, rows, cols[hh]]
                kb_ref[hh, jb] = kf.astype(BF16)
                vt_ref[hh, jb] = v_ref[0, rows, cols[hh]].T.astype(BF16)
                km_ref[hh, jb:jb + 1, :] = jnp.mean(kf, axis=0, keepdims=True)

    qs = [q_ref[0, :, c] for c in cols]
    for hh in range(hg):
        gate = _nt_dot_3pass(km_ref[hh], qs[hh])
        blk_id = lax.broadcasted_iota(jnp.int32, gate.shape, 0)
        rank = jnp.zeros(gate.shape, jnp.int32)
        for jp in range(nb):
            row = gate[jp:jp + 1, :]
            beats = (row > gate) | ((row == gate) & (jp < blk_id))
            rank = rank + jnp.where(beats & (jp < i), 1, 0)
        sel_ref[hh] = jnp.where((blk_id < i) & (rank < topk), 0.0, NEG_BIG)

    qb = [(q * (scale * LOG2E)).astype(BF16) for q in qs]

    def scores(hh, j):
        return lax.dot_general(kb_ref[hh, j], qb[hh], _NT, preferred_element_type=F32)

    def pv(hh, p, j):
        return jnp.dot(vt_ref[hh, j], p.astype(BF16), preferred_element_type=F32)

    kpos = lax.broadcasted_iota(jnp.int32, (blk, blk), 0)
    qpos = lax.broadcasted_iota(jnp.int32, (blk, blk), 1)
    causal = kpos <= qpos
    heads = range(hg)
    ss = [jnp.where(causal, scores(hh, i), NEG_BIG) for hh in heads]
    ms = [jnp.max(s, axis=0, keepdims=True) for s in ss]
    ps = [jnp.exp2(s - m) for s, m in zip(ss, ms)]
    ls = [jnp.sum(p, axis=0, keepdims=True) for p in ps]
    accs = [pv(hh, ps[hh], i) for hh in heads]

    for hh in heads:
        acc_ref[hh] = accs[hh]

    def produce(j, s_ref):
        mx = []
        for hh in heads:
            s = scores(hh, jnp.minimum(j, nb - 1))
            s_ref[hh] = s
            mx.append(jnp.max(s, axis=0, keepdims=True))
        return tuple(mx)

    def consume(j, s_ref, ms, ls, mx):
        sel = [sel_ref[hh, pl.ds(j, 1), :] for hh in heads]
        m_new = [jnp.maximum(m, x + b) for m, x, b in zip(ms, mx, sel)]
        alpha = [jnp.exp2(m - mn) for m, mn in zip(ms, m_new)]
        shift = [mn - 2.0 * b for mn, b in zip(m_new, sel)]
        ps = [jnp.exp2(s_ref[hh] - shift[hh]) for hh in heads]
        ls = [a * l + jnp.sum(p, axis=0, keepdims=True) for a, l, p in zip(alpha, ls, ps)]
        pvs = [pv(hh, ps[hh], j) for hh in heads]
        for hh in heads:
            acc_ref[hh] = alpha[hh] * acc_ref[hh] + pvs[hh]
        return tuple(m_new), tuple(ls)

    def past(jj, carry):
        ms, ls, mx0 = carry
        j0 = 2 * jj
        mx1 = produce(j0 + 1, s1_ref)
        ms, ls = consume(j0, s0_ref, ms, ls, mx0)
        mx0 = produce(j0 + 2, s0_ref)
        ms, ls = consume(j0 + 1, s1_ref, ms, ls, mx1)
        return ms, ls, mx0

    mx0 = produce(0, s0_ref)
    _, ls, _ = lax.fori_loop(0, (i + 1) // 2, past, (tuple(ms), tuple(ls), mx0))
    for hh in heads:
        o_ref[0, :, cols[hh]] = (acc_ref[hh] / ls[hh]).T.astype(o_ref.dtype)


def _moba(z, *, heads, hg):
    b, s, _ = z.shape
    dh, blk = A_HEAD_DIM, MOBA_BLOCK
    nb = s // blk
    nbp = -(-nb // SUBLANES) * SUBLANES
    ng = heads // hg
    assert s % blk == 0 and heads % hg == 0
    return pl.pallas_call(
        functools.partial(_moba_body, hg=hg, nb=nb, blk=blk, topk=min(MOBA_TOPK, nb), scale=dh ** -0.5),
        grid=(b, ng, nb),
        in_specs=[
            pl.BlockSpec((1, blk, hg * dh), lambda bi, g, i: (bi, i, g)),
            pl.BlockSpec((1, s, hg * dh), lambda bi, g, i: (bi, 0, ng + g)),
            pl.BlockSpec((1, s, hg * dh), lambda bi, g, i: (bi, 0, 2 * ng + g)),
        ],
        out_specs=pl.BlockSpec((1, blk, hg * dh), lambda bi, g, i: (bi, i, g)),
        out_shape=jax.ShapeDtypeStruct((b, s, heads * dh), BF16),
        scratch_shapes=[pltpu.VMEM((hg, nb, blk, dh), BF16), pltpu.VMEM((hg, nb, dh, blk), BF16),
                        pltpu.VMEM((hg, nbp, dh), F32), pltpu.VMEM((hg, nbp, blk), F32),
                        pltpu.VMEM((hg, blk, blk), F32), pltpu.VMEM((hg, blk, blk), F32),
                        pltpu.VMEM((hg, dh, blk), F32)],
        compiler_params=_params("parallel", "parallel", "arbitrary"),
        name="moba",
    )(z, z, z)


def _split_dot(x, ones_bd):
    hi = x.astype(BF16)
    lo = (x - hi.astype(F32)).astype(BF16)
    return (jnp.dot(hi, ones_bd, preferred_element_type=F32)
            + jnp.dot(lo, ones_bd, preferred_element_type=F32))


def _head_sum(x, ones_bd):
    cols = [_split_dot(x[:, c:c + LANES], ones_bd) for c in range(0, x.shape[-1], LANES)]
    return cols[0] if len(cols) == 1 else jnp.concatenate(cols, axis=-1)


def _head_ones():
    r = lax.broadcasted_iota(jnp.int32, (LANES, LANES), 0) // B_HEAD_DIM
    c = lax.broadcasted_iota(jnp.int32, (LANES, LANES), 1) // B_HEAD_DIM
    return jnp.where(r == c, 1.0, 0.0).astype(BF16)


def _prep_body(zr_ref, zk_ref, zv_ref, zl_ref, pr_ref, pk_ref, pv_ref, plo_ref,
               mur_ref, muk_ref, muv_ref, mul_ref, w0_ref, a0_ref, kk_ref, ka_ref, rk_ref,
               w2_ref, a2_ref, g2_ref, tri_ref, ones_ref,
               at_out, rp_out, bm_out, km_out, bc_out, kc_out, v_out, pc_out, bonus_out, g_out):
    ts = zr_ref.shape[1]
    first = pl.program_id(1) == 0
    row8 = lax.broadcasted_iota(jnp.int32, (SUBLANES, 1), 0)

    def lerp(z_ref, prev_ref, mu_ref):
        x = z_ref[0]
        prev = jnp.where(first, 0.0, prev_ref[0, SUBLANES - 1:SUBLANES, :])
        rolled = pltpu.roll(x, 1, axis=0)
        head = jnp.where(row8 == 0, prev, rolled[0:SUBLANES])
        shifted = jnp.concatenate([head, rolled[SUBLANES:]], axis=0)
        return x + (shifted - x) * mu_ref[...]

    r = lerp(zr_ref, pr_ref, mur_ref)
    k = lerp(zk_ref, pk_ref, muk_ref)
    v = lerp(zv_ref, pv_ref, muv_ref)
    lo = lerp(zl_ref, plo_ref, mul_ref)
    wa = lo[:, 0:LANES]
    gl = lo[:, LANES:2 * LANES]

    w_in = w0_ref[...] + jnp.dot(jnp.tanh(wa).astype(BF16), w2_ref[...], preferred_element_type=F32)
    lw = -math.exp(-0.5) * jax.nn.sigmoid(w_in)
    a = jax.nn.sigmoid(a0_ref[...] + jnp.dot(wa.astype(BF16), a2_ref[...], preferred_element_type=F32))
    g = jnp.dot(jax.nn.sigmoid(gl).astype(BF16), g2_ref[...], preferred_element_type=F32)

    ones_bd = ones_ref[...]
    kk = k * kk_ref[...]
    kk = kk * lax.rsqrt(jnp.maximum(_head_sum(kk * kk, ones_bd), 1e-24))
    k2 = k * (1.0 + (a - 1.0) * ka_ref[...])
    bonus = _head_sum(r * k2 * rk_ref[...], ones_bd) * v

    c_len = RWKV_CHUNK
    nc = ts // c_len
    bw = lw.shape[1]
    lw2 = lw * LOG2E
    y = jnp.dot(tri_ref[...], jnp.concatenate(_split_bf16(lw2, 3), axis=1), preferred_element_type=F32)
    cum = y[:, 0:bw] + y[:, bw:2 * bw] + y[:, 2 * bw:]
    last = cum.reshape(nc, c_len, bw)[:, c_len - 1:c_len, :]
    total = jnp.broadcast_to(last, (nc, c_len, bw)).reshape(ts, bw)
    p_inv = jnp.exp2(-cum)
    to_end = jnp.exp2(total - cum)
    b = kk * a
    at_out[0] = (-kk * jnp.exp2(cum - lw2)).astype(BF16)
    rp_out[0] = (r * jnp.exp2(cum)).astype(BF16)
    bm_out[0] = (b * p_inv).astype(BF16)
    km_out[0] = (k2 * p_inv).astype(BF16)
    bc_out[0] = (b * to_end).astype(BF16)
    kc_out[0] = (k2 * to_end).astype(BF16)
    v_out[0] = v.astype(BF16)
    pc_out[0] = jnp.exp2(jnp.broadcast_to(last, (nc, SUBLANES, bw)))
    bonus_out[0] = bonus
    g_out[0] = g


def _rwkv_prep(z, p, *, zb_col, ts):
    b, s, _ = z.shape
    bw = p["w0"].shape[-1]
    lw = 2 * LANES
    cb = zb_col // bw
    lb = (zb_col + 3 * bw) // lw
    assert zb_col % bw == 0 and (zb_col + 3 * bw) % lw == 0 and s % ts == 0 and ts % SUBLANES == 0
    sub = ts // SUBLANES

    def cur(col, width):
        return pl.BlockSpec((1, ts, width), lambda bi, t: (bi, t, col))

    def prev(col, width):
        return pl.BlockSpec((1, SUBLANES, width), lambda bi, t: (bi, jnp.maximum(t * sub - 1, 0), col))

    def vec(width):
        return pl.BlockSpec((1, width), lambda bi, t: (0, 0))

    def mat(width):
        return pl.BlockSpec((LANES, width), lambda bi, t: (0, 0))

    assert ts % RWKV_CHUNK == 0
    nc = ts // RWKV_CHUNK
    ti = jnp.arange(ts)
    tri = ((ti[:, None] // RWKV_CHUNK == ti[None, :] // RWKV_CHUNK) & (ti[None, :] <= ti[:, None])).astype(BF16)
    li = jnp.arange(LANES)
    ones_bd = (li[:, None] // B_HEAD_DIM == li[None, :] // B_HEAD_DIM).astype(BF16)

    def whole(shape):
        return pl.BlockSpec(shape, lambda bi, t: (0, 0))

    out_spec = pl.BlockSpec((1, ts, bw), lambda bi, t: (bi, t, 0))
    pc_spec = pl.BlockSpec((1, nc, SUBLANES, bw), lambda bi, t: (bi, t, 0, 0))
    tok16 = jax.ShapeDtypeStruct((b, s, bw), BF16)
    tok32 = jax.ShapeDtypeStruct((b, s, bw), F32)
    pc_shape = jax.ShapeDtypeStruct((b, s // RWKV_CHUNK, SUBLANES, bw), F32)
    return pl.pallas_call(
        _prep_body,
        grid=(b, s // ts),
        in_specs=[cur(cb, bw), cur(cb + 1, bw), cur(cb + 2, bw), cur(lb, lw),
                  prev(cb, bw), prev(cb + 1, bw), prev(cb + 2, bw), prev(lb, lw),
                  vec(bw), vec(bw), vec(bw), vec(lw), vec(bw), vec(bw), vec(bw), vec(bw), vec(bw),
                  mat(bw), mat(bw), mat(bw),
                  whole((ts, ts)), whole((LANES, LANES))],
        out_specs=[out_spec] * 7 + [pc_spec, out_spec, out_spec],
        out_shape=[tok16] * 7 + [pc_shape, tok32, tok32],
        compiler_params=_params("parallel", "arbitrary"),
        name="rwkv_prep",
    )(z, z, z, z, z, z, z, z,
      p["mu_r"], p["mu_k"], p["mu_v"], p["mu_l"], p["w0"], p["a0"], p["k_k"], p["k_a"], p["r_k"],
      p["w2"], p["a2"], p["g2"], tri, ones_bd)


def _bdot(a, b):
    return jnp.dot(a.astype(BF16), b.astype(BF16), preferred_element_type=F32)


def _scan_body(at_ref, rp_ref, bm_ref, km_ref, bc_ref, kc_ref, v_ref, pc_ref, bonus_ref, g_ref, gng_ref, gnb_ref,
               o_ref, h_ref, *, chunks, pairs, unroll):
    c_len = RWKV_CHUNK
    hd = B_HEAD_DIM
    prs = range(pairs)
    lanes = [slice(p * LANES, (p + 1) * LANES) for p in prs]

    @pl.when(pl.program_id(1) == 0)
    def _():
        h_ref[...] = jnp.zeros_like(h_ref)

    lane = lax.broadcasted_iota(jnp.int32, (c_len, LANES), 1)
    rowc = lax.broadcasted_iota(jnp.int32, (c_len, LANES), 0)
    head0 = lane < hd
    scol = lane % hd
    strict = scol < rowc
    incl = scol <= rowc
    r2 = lax.broadcasted_iota(jnp.int32, (LANES, LANES), 0)
    c2 = lax.broadcasted_iota(jnp.int32, (LANES, LANES), 1)
    bd_mask = (r2 // hd) == (c2 // hd)
    eye_cat = jnp.where(scol == rowc, 1.0, 0.0)
    ones_bd = _head_ones()
    gng = [gng_ref[:, ln] for ln in lanes]
    gnb = [gnb_ref[:, ln] for ln in lanes]

    def stack(x):
        return jnp.concatenate([jnp.where(head0, x, 0.0), jnp.where(head0, 0.0, x)], axis=0)

    def chunk_step(c, carry):
        rows = [pl.ds(pl.multiple_of((c * unroll + ci) * c_len, c_len), c_len) for ci in range(unroll)]
        units = [(ci, p) for ci in range(unroll) for p in prs]
        us = range(len(units))

        def load(ref):
            return [ref[0, rows[ci], lanes[p]] for ci, p in units]

        at, rp, bm, km, v = load(at_ref), load(rp_ref), load(bm_ref), load(km_ref), load(v_ref)
        bc_t = [x.astype(F32).T for x in load(bc_ref)]
        kc_t = [x.astype(F32).T for x in load(kc_ref)]
        pc_col = [pc_ref[0, c * unroll + ci, :, lanes[p]].T[:, 0:1] for ci, p in units]

        lhs = [jnp.concatenate([at[u], rp[u]], axis=0) for u in us]
        rhs = [jnp.concatenate([stack(bm[u]), stack(km[u])], axis=0) for u in us]
        amat = [lax.dot_general(lhs[u], rhs[u], _NT, preferred_element_type=F32) for u in us]
        a_ab = [jnp.where(strict, m[0:c_len, 0:LANES], 0.0) for m in amat]
        a_ak = [jnp.where(strict, m[0:c_len, LANES:2 * LANES], 0.0) for m in amat]
        a_rb = [jnp.where(incl, m[c_len:, 0:LANES], 0.0) for m in amat]
        a_rk = [jnp.where(incl, m[c_len:, LANES:2 * LANES], 0.0) for m in amat]

        pw = a_ab
        t_inv = [eye_cat + x for x in a_ab]
        pw_bd = [stack(x) for x in pw]
        pw = [_bdot(pw[u], pw_bd[u]) for u in us]
        for _ in range(int(math.log2(c_len)) - 2):
            pw_bd = [stack(x) for x in pw]
            both = [_bdot(jnp.concatenate([pw[u], t_inv[u]], axis=0), pw_bd[u]) for u in us]
            pw = [x[0:c_len] for x in both]
            t_inv = [t_inv[u] + both[u][c_len:] for u in us]
        pw_bd = [stack(x) for x in pw]
        t_inv = [t_inv[u] + _bdot(t_inv[u], pw_bd[u]) for u in us]

        v_st = [stack(x) for x in v]
        akv = [_bdot(a_ak[u], v_st[u]) for u in us]
        sol = [_bdot(t_inv[u], jnp.concatenate([stack(at[u]), stack(akv[u])], axis=1)) for u in us]
        y0 = [_bdot(a_rk[u], v_st[u]) for u in us]
        hkv = [jnp.where(bd_mask, _bdot(kc_t[u], v[u]), 0.0) for u in us]

        h = [h_ref[p] for p in prs]
        ys = []
        for ci in range(unroll):
            un = [ci * pairs + p for p in prs]
            hb = [x.astype(BF16) for x in h]
            uu = [jnp.dot(sol[un[p]][:, 0:LANES].astype(BF16), hb[p], preferred_element_type=F32)
                  + sol[un[p]][:, LANES:] for p in prs]
            y = [jnp.dot(rp[un[p]].astype(BF16), hb[p], preferred_element_type=F32) + y0[un[p]] for p in prs]
            ys += [y[p] + _bdot(a_rb[un[p]], stack(uu[p])) for p in prs]
            h = [pc_col[un[p]] * h[p] + jnp.where(bd_mask, _bdot(bc_t[un[p]], uu[p]), 0.0) + hkv[un[p]]
                 for p in prs]
        for p in prs:
            h_ref[p] = h[p]

        mean = [_split_dot(x, ones_bd) * (1.0 / hd) for x in ys]
        dlt = [ys[u] - mean[u] for u in us]
        var = [_split_dot(x * x, ones_bd) * (1.0 / hd) for x in dlt]
        for u, (ci, p) in enumerate(units):
            yn = dlt[u] * lax.rsqrt(var[u] + GN_EPS) * gng[p] + gnb[p]
            out = (yn + bonus_ref[0, rows[ci], lanes[p]]) * g_ref[0, rows[ci], lanes[p]]
            o_ref[0, rows[ci], lanes[p]] = out.astype(o_ref.dtype)
        return carry

    lax.fori_loop(0, chunks // unroll, chunk_step, 0)


def _rwkv_scan(prep, gn_g, gn_b, *, ts):
    b, s, bw = prep[0].shape
    assert ts % (RWKV_CHUNK * RWKV_UNROLL) == 0 and s % ts == 0 and bw % LANES == 0
    pairs = bw // LANES
    tok = pl.BlockSpec((1, ts, bw), lambda bi, t: (bi, t, 0))
    pcs = pl.BlockSpec((1, ts // RWKV_CHUNK, SUBLANES, bw), lambda bi, t: (bi, t, 0, 0))
    vec = pl.BlockSpec((1, bw), lambda bi, t: (0, 0))
    return pl.pallas_call(
        functools.partial(_scan_body, chunks=ts // RWKV_CHUNK, pairs=pairs, unroll=RWKV_UNROLL),
        grid=(b, s // ts),
        in_specs=[tok] * 7 + [pcs, tok, tok, vec, vec],
        out_specs=tok,
        out_shape=jax.ShapeDtypeStruct((b, s, bw), BF16),
        scratch_shapes=[pltpu.VMEM((pairs, LANES, LANES), F32)],
        compiler_params=_params("parallel", "arbitrary"),
        name="rwkv_scan",
    )(*prep, gn_g, gn_b)


def _outproj_body(h_ref, ya_ref, yb_ref, wa_ref, wb_ref, o_ref):
    acc = jnp.dot(ya_ref[...].astype(BF16), wa_ref[...], preferred_element_type=F32)
    acc = acc + jnp.dot(yb_ref[...].astype(BF16), wb_ref[...], preferred_element_type=F32)
    o_ref[...] = h_ref[...] + acc


def _outproj(h, ya, yb, w_out, e, *, tm, tn):
    t, d = h.shape
    wa, wb = ya.shape[-1], yb.shape[-1]
    assert wa == wb
    return pl.pallas_call(
        _outproj_body,
        grid=(t // tm, d // tn),
        in_specs=[
            pl.BlockSpec((tm, tn), lambda i, j: (i, j)),
            pl.BlockSpec((tm, wa), lambda i, j: (i, 0)),
            pl.BlockSpec((tm, wb), lambda i, j: (i, 0)),
            pl.BlockSpec((None, wa, tn), lambda i, j: (e, 0, j)),
            pl.BlockSpec((None, wb, tn), lambda i, j: (e, 1, j)),
        ],
        out_specs=pl.BlockSpec((tm, tn), lambda i, j: (i, j)),
        out_shape=jax.ShapeDtypeStruct((t, d), F32),
        compiler_params=_params("parallel", "arbitrary"),
        name="outproj",
    )(h, ya, yb, w_out, w_out)


def _pool_body(h_ref, halo_ref, g_ref, w_ref, sc_ref, o_ref, *, windows, halo):
    ts, d = h_ref.shape[1], h_ref.shape[2]
    grp = d // len(windows)
    t = pl.program_id(1)
    hcur = h_ref[0]
    u = _rms(hcur, g_ref[...])
    uh = jnp.where(t == 0, 0.0, _rms(halo_ref[0], g_ref[...]))
    pos = t * ts + lax.broadcasted_iota(jnp.int32, (ts, 1), 0)
    outs = []
    for gi, win in enumerate(windows):
        cols = slice(gi * grp, (gi + 1) * grp)
        ug = u[:, cols]
        ext = jnp.concatenate([uh[:, cols], ug], axis=0)
        span = 1
        while span < win:
            ext = ext + pltpu.roll(ext, span, axis=0)
            span *= 2
        cnt = jnp.minimum(pos + 1, win).astype(F32)
        diff = ext[halo:, :] / cnt - ug
        outs.append(jnp.dot(diff.astype(BF16), w_ref[gi], preferred_element_type=F32))
    o_ref[0] = hcur + jnp.concatenate(outs, axis=-1) * sc_ref[...]


def _pool(h, gain, w_grp, o, scale, *, ts):
    b, s, d = h.shape
    halo = 2 * SUBLANES
    assert max(POOL_WINDOWS) <= halo and all(w & (w - 1) == 0 for w in POOL_WINDOWS)
    assert s % ts == 0 and ts % halo == 0
    ng, grp = w_grp.shape[1], w_grp.shape[2]
    sub = ts // halo
    return pl.pallas_call(
        functools.partial(_pool_body, windows=POOL_WINDOWS, halo=halo),
        grid=(b, s // ts),
        in_specs=[
            pl.BlockSpec((1, ts, d), lambda bi, t: (bi, t, 0)),
            pl.BlockSpec((1, halo, d), lambda bi, t: (bi, jnp.maximum(t * sub - 1, 0), 0)),
            pl.BlockSpec((1, d), lambda bi, t: (0, 0)),
            pl.BlockSpec((None, ng, grp, grp), lambda bi, t: (o, 0, 0, 0)),
            pl.BlockSpec((1, d), lambda bi, t: (0, 0)),
        ],
        out_specs=pl.BlockSpec((1, ts, d), lambda bi, t: (bi, t, 0)),
        out_shape=jax.ShapeDtypeStruct((b, s, d), F32),
        compiler_params=_params("parallel", "arbitrary"),
        name="pool",
    )(h, h, gain, w_grp, scale)


def _pick(n, pref):
    for c in range(min(pref, n), 0, -1):
        if n % c == 0 and (c % SUBLANES == 0 or c == n):
            return c
    return n


def kernel(x, ffn1_norm, ffn1_wg, ffn1_wu, ffn1_wd, mix_norm, ffn2_norm, ffn2_wg, ffn2_wu, ffn2_wd,
           ab_w_in, ab_w_out, rwkv_mu, rwkv_w0, rwkv_w2, rwkv_a0, rwkv_a2, rwkv_g2, rwkv_k_k, rwkv_k_a,
           rwkv_r_k, rwkv_gn_g, rwkv_gn_b, pool_w, pool_scale, final_norm):
    b, s, d = x.shape
    t = b * s
    depth = ffn1_norm.shape[0]
    bw = rwkv_w0.shape[-1]
    aw = ab_w_out.shape[1] - bw
    a_heads = aw // A_HEAD_DIM
    zb_col = 3 * aw
    assert rwkv_w2.shape[1] == LORA and 3 * LORA <= 2 * LANES

    ffn_w = [(ffn1_wg, ffn1_wu, ffn1_wd), (ffn2_wg, ffn2_wu, ffn2_wd)]
    cur_w = tuple(w[0].astype(BF16) for w in ffn_w[0])
    n_in = ab_w_in.shape[-1]
    n_pad = zb_col + 3 * bw + 2 * LANES - n_in
    w_in = jnp.pad(ab_w_in, ((0, 0), (0, 0), (0, n_pad))).astype(BF16)
    w_out = ab_w_out.astype(BF16)
    pool_wb = pool_w.astype(BF16)
    zrow = jnp.zeros((LANES - LORA, bw), F32)

    tm = _pick(t, TOKEN_TILE)
    tmo = _pick(t, OUTPROJ_TOKEN_TILE)
    tf = _pick(ffn1_wg.shape[-1], FF_TILE)
    row = lambda v: v.reshape(1, -1)

    h = x.reshape(t, d)
    for layer in range(depth):
        h, cur_w = _ffn(h, row(ffn1_norm[layer]), *cur_w, row(final_norm), (*ffn_w[1], layer),
                        final=False, tm=tm, tf=tf)
        if layer % 2 == 0:
            e = layer // 2
            z = _inproj(h, row(mix_norm[layer]), w_in, e, tm=tm, tn=_pick(w_in.shape[-1], INPROJ_COL_TILE))
            z = z.reshape(b, s, -1)
            ya = _moba(z, heads=a_heads, hg=math.gcd(a_heads, MOBA_HEAD_GROUP))
            mu = rwkv_mu[e]
            prm = {
                "mu_r": row(mu[0:bw]), "mu_k": row(mu[bw:2 * bw]), "mu_v": row(mu[2 * bw:3 * bw]),
                "mu_l": row(jnp.pad(mu[3 * bw:], (0, n_pad))),
                "w0": row(rwkv_w0[e]), "a0": row(rwkv_a0[e]), "k_k": row(rwkv_k_k[e]),
                "k_a": row(rwkv_k_a[e]), "r_k": row(rwkv_r_k[e]),
                "w2": jnp.concatenate([rwkv_w2[e], zrow], axis=0).astype(BF16),
                "a2": jnp.concatenate([zrow, rwkv_a2[e]], axis=0).astype(BF16),
                "g2": jnp.concatenate([rwkv_g2[e], zrow], axis=0).astype(BF16),
            }
            prep = _rwkv_prep(z, prm, zb_col=zb_col, ts=_pick(s, RWKV_PREP_TILE))
            yb = _rwkv_scan(prep, row(rwkv_gn_g[e]), row(rwkv_gn_b[e]), ts=_pick(s, RWKV_SCAN_TILE))
            h = _outproj(h, ya.reshape(t, aw), yb.reshape(t, bw), w_out, e, tm=tmo, tn=d)
        else:
            o = layer // 2
            h = _pool(h.reshape(b, s, d), row(mix_norm[layer]), pool_wb, o, row(pool_scale[o]),
                      ts=_pick(s, POOL_TILE)).reshape(t, d)
        if layer + 1 < depth:
            h, cur_w = _ffn(h, row(ffn2_norm[layer]), *cur_w, row(final_norm), (*ffn_w[0], layer + 1),
                            final=False, tm=tm, tf=tf)
        else:
            h = _ffn(h, row(ffn2_norm[layer]), *cur_w, row(final_norm), None, final=True, tm=tm, tf=tf)
    return h.reshape(b, s, d)
```

```python
import functools
import math

import jax
import jax.numpy as jnp
from jax import lax
from jax.experimental import pallas as pl
from jax.experimental.pallas import tpu as pltpu

F32 = jnp.float32
BF16 = jnp.bfloat16

A_HEAD_DIM = 128
MOBA_BLOCK = 256
MOBA_TOPK = 3
B_HEAD_DIM = 64
LORA = 64
GN_EPS = 64e-5
RMS_EPS = 1e-6
POOL_WINDOWS = (2, 4, 8, 16)

LANES = 128
SUBLANES = 8
VMEM_LIMIT_BYTES = 56 * 1024 * 1024
FFN_VMEM_LIMIT_BYTES = 58 * 1024 * 1024

LOG2E = math.log2(math.e)
NEG_BIG = -1e30
MOBA_HEAD_GROUP = 4
RWKV_UNROLL = 2
RWKV_CHUNK = 64

TOKEN_TILE = 1024
FF_TILE = 512
INPROJ_COL_TILE = 1280
OUTPROJ_TOKEN_TILE = 512
RWKV_PREP_TILE = 512
RWKV_SCAN_TILE = 512
POOL_TILE = 512

_NT = (((1,), (1,)), ((), ()))


def _rms(x, gain):
    ms = jnp.mean(x * x, axis=-1, keepdims=True)
    return x * lax.rsqrt(ms + RMS_EPS) * gain


def _params(*sem, vmem=VMEM_LIMIT_BYTES):
    return pltpu.CompilerParams(dimension_semantics=sem, vmem_limit_bytes=vmem)


def _ffn_body(h_ref, g_ref, wg_ref, wu_ref, wd_ref, fg_ref, o_ref, n_ref, *, final):
    f = pl.program_id(1)

    @pl.when(f == 0)
    def _():
        h = h_ref[...]
        n_ref[...] = _rms(h, g_ref[...]).astype(BF16)
        o_ref[...] = h

    n = n_ref[...]
    gate = jnp.dot(n, wg_ref[...], preferred_element_type=F32)
    up = jnp.dot(n, wu_ref[...], preferred_element_type=F32)
    act = (gate * jax.nn.sigmoid(gate) * up).astype(BF16)
    o_ref[...] += 0.5 * jnp.dot(act, wd_ref[...], preferred_element_type=F32)

    if final:
        @pl.when(f == pl.num_programs(1) - 1)
        def _():
            o_ref[...] = _rms(o_ref[...], fg_ref[...])


def _ffn(h, gain, wg, wu, wd, final_gain, *, final, tm, tf):
    t, d = h.shape
    f = wg.shape[-1]
    return pl.pallas_call(
        functools.partial(_ffn_body, final=final),
        grid=(t // tm, f // tf),
        in_specs=[
            pl.BlockSpec((tm, d), lambda i, j: (i, 0)),
            pl.BlockSpec((1, d), lambda i, j: (0, 0)),
            pl.BlockSpec((d, tf), lambda i, j: (0, j)),
            pl.BlockSpec((d, tf), lambda i, j: (0, j)),
            pl.BlockSpec((tf, d), lambda i, j: (j, 0)),
            pl.BlockSpec((1, d), lambda i, j: (0, 0)),
        ],
        out_specs=pl.BlockSpec((tm, d), lambda i, j: (i, 0)),
        out_shape=jax.ShapeDtypeStruct((t, d), F32),
        scratch_shapes=[pltpu.VMEM((tm, d), BF16)],
        compiler_params=_params("parallel", "arbitrary", vmem=FFN_VMEM_LIMIT_BYTES),
        name="ffn",
    )(h, gain, wg, wu, wd, final_gain)


def _cast_specs(jobs, steps, flat):
    in_specs, out_specs, out_shapes, args = [], [], [], []
    for w, layer in jobs:
        _, r, c = w.shape
        rows = r // steps
        assert r % steps == 0 and rows % (2 * SUBLANES) == 0
        in_specs.append(pl.BlockSpec((None, rows, c), lambda *ids, layer=layer: (layer, flat(*ids), 0)))
        out_specs.append(pl.BlockSpec((rows, c), lambda *ids: (flat(*ids), 0)))
        out_shapes.append(jax.ShapeDtypeStruct((r, c), BF16))
        args.append(w)
    return in_specs, out_specs, out_shapes, args


def _inproj_body(h_ref, g_ref, w_ref, z_ref, n_ref):
    @pl.when(pl.program_id(1) == 0)
    def _():
        n_ref[...] = _rms(h_ref[...], g_ref[...]).astype(BF16)

    z_ref[...] = jnp.dot(n_ref[...], w_ref[...], preferred_element_type=F32)


def _inproj(h, gain, w_in, e, *, tm, tn):
    t, d = h.shape
    n = w_in.shape[-1]
    return pl.pallas_call(
        _inproj_body,
        grid=(t // tm, n // tn),
        in_specs=[
            pl.BlockSpec((tm, d), lambda i, j: (i, 0)),
            pl.BlockSpec((1, d), lambda i, j: (0, 0)),
            pl.BlockSpec((None, d, tn), lambda i, j: (e, 0, j)),
        ],
        out_specs=pl.BlockSpec((tm, tn), lambda i, j: (i, j)),
        out_shape=jax.ShapeDtypeStruct((t, n), F32),
        scratch_shapes=[pltpu.VMEM((tm, d), BF16)],
        compiler_params=_params("parallel", "arbitrary"),
        name="inproj",
    )(h, gain, w_in)


def _split_bf16(x, parts):
    out = []
    for _ in range(parts):
        p = x.astype(BF16)
        out.append(p)
        x = x - p.astype(F32)
    return out


def _nt_dot_3pass(a, b):
    ah, al = _split_bf16(a, 2)
    bh, bl = _split_bf16(b, 2)
    n = a.shape[0]
    both = lax.dot_general(jnp.concatenate([ah, al], axis=0), bh, _NT, preferred_element_type=F32)
    return both[0:n] + both[n:] + lax.dot_general(ah, bl, _NT, preferred_element_type=F32)


def _moba_body(q_ref, k_ref, v_ref, *rest, ncast, hg, nb, blk, topk, scale):
    cast_in, o_ref, cast_out = rest[:ncast], rest[ncast], rest[ncast + 1:2 * ncast + 1]
    kb_ref, vt_ref, km_ref, sel_ref, s0_ref, s1_ref, acc_ref = rest[2 * ncast + 1:]
    for src, dst in zip(cast_in, cast_out):
        dst[...] = src[...].astype(BF16)
    i = pl.program_id(2)
    dh = A_HEAD_DIM
    cols = [slice(hh * dh, (hh + 1) * dh) for hh in range(hg)]

    @pl.when(i == 0)
    def _():
        km_ref[...] = jnp.zeros_like(km_ref)
        for hh in range(hg):
            for jb in range(nb):
                rows = slice(jb * blk, (jb + 1) * blk)
                kf = k_ref[0, rows, cols[hh]]
                kb_ref[hh, jb] = kf.astype(BF16)
                vt_ref[hh, jb] = v_ref[0, rows, cols[hh]].T.astype(BF16)
                km_ref[hh, jb:jb + 1, :] = jnp.mean(kf, axis=0, keepdims=True)

    qs = [q_ref[0, :, c] for c in cols]
    for hh in range(hg):
        gate = _nt_dot_3pass(km_ref[hh], qs[hh])
        blk_id = lax.broadcasted_iota(jnp.int32, gate.shape, 0)
        rank = jnp.zeros(gate.shape, jnp.int32)
        for jp in range(nb):
            row = gate[jp:jp + 1, :]
            beats = (row > gate) | ((row == gate) & (jp < blk_id))
            rank = rank + jnp.where(beats & (jp < i), 1, 0)
        sel_ref[hh] = jnp.where((blk_id < i) & (rank < topk), 0.0, NEG_BIG)

    qb = [(q * (scale * LOG2E)).astype(BF16) for q in qs]

    def scores(hh, j):
        return lax.dot_general(kb_ref[hh, j], qb[hh], _NT, preferred_element_type=F32)

    def pv(hh, p, j):
        return jnp.dot(vt_ref[hh, j], p.astype(BF16), preferred_element_type=F32)

    kpos = lax.broadcasted_iota(jnp.int32, (blk, blk), 0)
    qpos = lax.broadcasted_iota(jnp.int32, (blk, blk), 1)
    causal = kpos <= qpos
    heads = range(hg)
    ss = [jnp.where(causal, scores(hh, i), NEG_BIG) for hh in heads]
    ms = [jnp.max(s, axis=0, keepdims=True) for s in ss]
    ps = [jnp.exp2(s - m) for s, m in zip(ss, ms)]
    ls = [jnp.sum(p, axis=0, keepdims=True) for p in ps]
    accs = [pv(hh, ps[hh], i) for hh in heads]

    for hh in heads:
        acc_ref[hh] = accs[hh]

    def produce(j, s_ref):
        mx = []
        for hh in heads:
            s = scores(hh, jnp.minimum(j, nb - 1))
            s_ref[hh] = s
            mx.append(jnp.max(s, axis=0, keepdims=True))
        return tuple(mx)

    def consume(j, s_ref, ms, ls, mx):
        sel = [sel_ref[hh, pl.ds(j, 1), :] for hh in heads]
        m_new = [jnp.maximum(m, x + b) for m, x, b in zip(ms, mx, sel)]
        alpha = [jnp.exp2(m - mn) for m, mn in zip(ms, m_new)]
        shift = [mn - 2.0 * b for mn, b in zip(m_new, sel)]
        ps = [jnp.exp2(s_ref[hh] - shift[hh]) for hh in heads]
        ls = [a * l + jnp.sum(p, axis=0, keepdims=True) for a, l, p in zip(alpha, ls, ps)]
        pvs = [pv(hh, ps[hh], j) for hh in heads]
        for hh in heads:
            acc_ref[hh] = alpha[hh] * acc_ref[hh] + pvs[hh]
        return tuple(m_new), tuple(ls)

    def past(jj, carry):
        ms, ls, mx0 = carry
        j0 = 2 * jj
        mx1 = produce(j0 + 1, s1_ref)
        ms, ls = consume(j0, s0_ref, ms, ls, mx0)
        mx0 = produce(j0 + 2, s0_ref)
        ms, ls = consume(j0 + 1, s1_ref, ms, ls, mx1)
        return ms, ls, mx0

    mx0 = produce(0, s0_ref)
    _, ls, _ = lax.fori_loop(0, (i + 1) // 2, past, (tuple(ms), tuple(ls), mx0))
    for hh in heads:
        o_ref[0, :, cols[hh]] = (acc_ref[hh] / ls[hh]).T.astype(o_ref.dtype)


def _moba(z, cast_jobs, *, heads, hg):
    b, s, _ = z.shape
    dh, blk = A_HEAD_DIM, MOBA_BLOCK
    nb = s // blk
    nbp = -(-nb // SUBLANES) * SUBLANES
    ng = heads // hg
    assert s % blk == 0 and heads % hg == 0
    c_in, c_out, c_shape, c_args = _cast_specs(cast_jobs, b * ng * nb, lambda bi, g, i: (bi * ng + g) * nb + i)
    outs = pl.pallas_call(
        functools.partial(_moba_body, ncast=len(cast_jobs), hg=hg, nb=nb, blk=blk, topk=min(MOBA_TOPK, nb),
                          scale=dh ** -0.5),
        grid=(b, ng, nb),
        in_specs=[
            pl.BlockSpec((1, blk, hg * dh), lambda bi, g, i: (bi, i, g)),
            pl.BlockSpec((1, s, hg * dh), lambda bi, g, i: (bi, 0, ng + g)),
            pl.BlockSpec((1, s, hg * dh), lambda bi, g, i: (bi, 0, 2 * ng + g)),
        ] + c_in,
        out_specs=[pl.BlockSpec((1, blk, hg * dh), lambda bi, g, i: (bi, i, g))] + c_out,
        out_shape=[jax.ShapeDtypeStruct((b, s, heads * dh), BF16)] + c_shape,
        scratch_shapes=[pltpu.VMEM((hg, nb, blk, dh), BF16), pltpu.VMEM((hg, nb, dh, blk), BF16),
                        pltpu.VMEM((hg, nbp, dh), F32), pltpu.VMEM((hg, nbp, blk), F32),
                        pltpu.VMEM((hg, blk, blk), F32), pltpu.VMEM((hg, blk, blk), F32),
                        pltpu.VMEM((hg, dh, blk), F32)],
        compiler_params=_params("parallel", "parallel", "arbitrary"),
        name="moba",
    )(z, z, z, *c_args)
    return outs[0], outs[1:]


def _split_dot(x, ones_bd):
    hi = x.astype(BF16)
    lo = (x - hi.astype(F32)).astype(BF16)
    return (jnp.dot(hi, ones_bd, preferred_element_type=F32)
            + jnp.dot(lo, ones_bd, preferred_element_type=F32))


def _head_sum(x, ones_bd):
    cols = [_split_dot(x[:, c:c + LANES], ones_bd) for c in range(0, x.shape[-1], LANES)]
    return cols[0] if len(cols) == 1 else jnp.concatenate(cols, axis=-1)


def _head_ones():
    r = lax.broadcasted_iota(jnp.int32, (LANES, LANES), 0) // B_HEAD_DIM
    c = lax.broadcasted_iota(jnp.int32, (LANES, LANES), 1) // B_HEAD_DIM
    return jnp.where(r == c, 1.0, 0.0).astype(BF16)


def _prep_body(zr_ref, zk_ref, zv_ref, zl_ref, pr_ref, pk_ref, pv_ref, plo_ref,
               mur_ref, muk_ref, muv_ref, mul_ref, w0_ref, a0_ref, kk_ref, ka_ref, rk_ref,
               w2_ref, a2_ref, g2_ref, tri_ref, ones_ref,
               at_out, rp_out, bm_out, km_out, bc_out, kc_out, v_out, pc_out, bonus_out, g_out):
    ts = zr_ref.shape[1]
    first = pl.program_id(1) == 0
    row8 = lax.broadcasted_iota(jnp.int32, (SUBLANES, 1), 0)

    def lerp(z_ref, prev_ref, mu_ref):
        x = z_ref[0]
        prev = jnp.where(first, 0.0, prev_ref[0, SUBLANES - 1:SUBLANES, :])
        rolled = pltpu.roll(x, 1, axis=0)
        head = jnp.where(row8 == 0, prev, rolled[0:SUBLANES])
        shifted = jnp.concatenate([head, rolled[SUBLANES:]], axis=0)
        return x + (shifted - x) * mu_ref[...]

    r = lerp(zr_ref, pr_ref, mur_ref)
    k = lerp(zk_ref, pk_ref, muk_ref)
    v = lerp(zv_ref, pv_ref, muv_ref)
    lo = lerp(zl_ref, plo_ref, mul_ref)
    wa = lo[:, 0:LANES]
    gl = lo[:, LANES:2 * LANES]

    w_in = w0_ref[...] + jnp.dot(jnp.tanh(wa).astype(BF16), w2_ref[...], preferred_element_type=F32)
    lw = -math.exp(-0.5) * jax.nn.sigmoid(w_in)
    a = jax.nn.sigmoid(a0_ref[...] + jnp.dot(wa.astype(BF16), a2_ref[...], preferred_element_type=F32))
    g = jnp.dot(jax.nn.sigmoid(gl).astype(BF16), g2_ref[...], preferred_element_type=F32)

    ones_bd = ones_ref[...]
    kk = k * kk_ref[...]
    kk = kk * lax.rsqrt(jnp.maximum(_head_sum(kk * kk, ones_bd), 1e-24))
    k2 = k * (1.0 + (a - 1.0) * ka_ref[...])
    bonus = _head_sum(r * k2 * rk_ref[...], ones_bd) * v

    c_len = RWKV_CHUNK
    nc = ts // c_len
    bw = lw.shape[1]
    lw2 = lw * LOG2E
    y = jnp.dot(tri_ref[...], jnp.concatenate(_split_bf16(lw2, 3), axis=1), preferred_element_type=F32)
    cum = y[:, 0:bw] + y[:, bw:2 * bw] + y[:, 2 * bw:]
    last = cum.reshape(nc, c_len, bw)[:, c_len - 1:c_len, :]
    total = jnp.broadcast_to(last, (nc, c_len, bw)).reshape(ts, bw)
    p_inv = jnp.exp2(-cum)
    to_end = jnp.exp2(total - cum)
    b = kk * a
    at_out[0] = (-kk * jnp.exp2(cum - lw2)).astype(BF16)
    rp_out[0] = (r * jnp.exp2(cum)).astype(BF16)
    bm_out[0] = (b * p_inv).astype(BF16)
    km_out[0] = (k2 * p_inv).astype(BF16)
    bc_out[0] = (b * to_end).astype(BF16)
    kc_out[0] = (k2 * to_end).astype(BF16)
    v_out[0] = v.astype(BF16)
    pc_out[0] = jnp.exp2(jnp.broadcast_to(last, (nc, SUBLANES, bw)))
    bonus_out[0] = bonus
    g_out[0] = g


def _rwkv_prep(z, p, *, zb_col, ts):
    b, s, _ = z.shape
    bw = p["w0"].shape[-1]
    lw = 2 * LANES
    cb = zb_col // bw
    lb = (zb_col + 3 * bw) // lw
    assert zb_col % bw == 0 and (zb_col + 3 * bw) % lw == 0 and s % ts == 0 and ts % SUBLANES == 0
    sub = ts // SUBLANES

    def cur(col, width):
        return pl.BlockSpec((1, ts, width), lambda bi, t: (bi, t, col))

    def prev(col, width):
        return pl.BlockSpec((1, SUBLANES, width), lambda bi, t: (bi, jnp.maximum(t * sub - 1, 0), col))

    def vec(width):
        return pl.BlockSpec((1, width), lambda bi, t: (0, 0))

    def mat(width):
        return pl.BlockSpec((LANES, width), lambda bi, t: (0, 0))

    assert ts % RWKV_CHUNK == 0
    nc = ts // RWKV_CHUNK
    ti = jnp.arange(ts)
    tri = ((ti[:, None] // RWKV_CHUNK == ti[None, :] // RWKV_CHUNK) & (ti[None, :] <= ti[:, None])).astype(BF16)
    li = jnp.arange(LANES)
    ones_bd = (li[:, None] // B_HEAD_DIM == li[None, :] // B_HEAD_DIM).astype(BF16)

    def whole(shape):
        return pl.BlockSpec(shape, lambda bi, t: (0, 0))

    out_spec = pl.BlockSpec((1, ts, bw), lambda bi, t: (bi, t, 0))
    pc_spec = pl.BlockSpec((1, nc, SUBLANES, bw), lambda bi, t: (bi, t, 0, 0))
    tok16 = jax.ShapeDtypeStruct((b, s, bw), BF16)
    tok32 = jax.ShapeDtypeStruct((b, s, bw), F32)
    pc_shape = jax.ShapeDtypeStruct((b, s // RWKV_CHUNK, SUBLANES, bw), F32)
    return pl.pallas_call(
        _prep_body,
        grid=(b, s // ts),
        in_specs=[cur(cb, bw), cur(cb + 1, bw), cur(cb + 2, bw), cur(lb, lw),
                  prev(cb, bw), prev(cb + 1, bw), prev(cb + 2, bw), prev(lb, lw),
                  vec(bw), vec(bw), vec(bw), vec(lw), vec(bw), vec(bw), vec(bw), vec(bw), vec(bw),
                  mat(bw), mat(bw), mat(bw),
                  whole((ts, ts)), whole((LANES, LANES))],
        out_specs=[out_spec] * 7 + [pc_spec, out_spec, out_spec],
        out_shape=[tok16] * 7 + [pc_shape, tok32, tok32],
        compiler_params=_params("parallel", "arbitrary"),
        name="rwkv_prep",
    )(z, z, z, z, z, z, z, z,
      p["mu_r"], p["mu_k"], p["mu_v"], p["mu_l"], p["w0"], p["a0"], p["k_k"], p["k_a"], p["r_k"],
      p["w2"], p["a2"], p["g2"], tri, ones_bd)


def _bdot(a, b):
    return jnp.dot(a.astype(BF16), b.astype(BF16), preferred_element_type=F32)


def _scan_body(at_ref, rp_ref, bm_ref, km_ref, bc_ref, kc_ref, v_ref, pc_ref, bonus_ref, g_ref, gng_ref, gnb_ref,
               *rest, ncast, chunks, pairs, unroll):
    cast_in, o_ref, cast_out, h_ref = rest[:ncast], rest[ncast], rest[ncast + 1:2 * ncast + 1], rest[2 * ncast + 1]
    for src, dst in zip(cast_in, cast_out):
        dst[...] = src[...].astype(BF16)
    c_len = RWKV_CHUNK
    hd = B_HEAD_DIM
    prs = range(pairs)
    lanes = [slice(p * LANES, (p + 1) * LANES) for p in prs]

    @pl.when(pl.program_id(1) == 0)
    def _():
        h_ref[...] = jnp.zeros_like(h_ref)

    lane = lax.broadcasted_iota(jnp.int32, (c_len, LANES), 1)
    rowc = lax.broadcasted_iota(jnp.int32, (c_len, LANES), 0)
    head0 = lane < hd
    scol = lane % hd
    strict = scol < rowc
    incl = scol <= rowc
    r2 = lax.broadcasted_iota(jnp.int32, (LANES, LANES), 0)
    c2 = lax.broadcasted_iota(jnp.int32, (LANES, LANES), 1)
    bd_mask = (r2 // hd) == (c2 // hd)
    eye_cat = jnp.where(scol == rowc, 1.0, 0.0)
    ones_bd = _head_ones()
    gng = [gng_ref[:, ln] for ln in lanes]
    gnb = [gnb_ref[:, ln] for ln in lanes]

    def stack(x):
        return jnp.concatenate([jnp.where(head0, x, 0.0), jnp.where(head0, 0.0, x)], axis=0)

    def chunk_step(c, carry):
        rows = [pl.ds(pl.multiple_of((c * unroll + ci) * c_len, c_len), c_len) for ci in range(unroll)]
        units = [(ci, p) for ci in range(unroll) for p in prs]
        us = range(len(units))

        def load(ref):
            return [ref[0, rows[ci], lanes[p]] for ci, p in units]

        at, rp, bm, km, v = load(at_ref), load(rp_ref), load(bm_ref), load(km_ref), load(v_ref)
        bc_t = [x.astype(F32).T for x in load(bc_ref)]
        kc_t = [x.astype(F32).T for x in load(kc_ref)]
        pc_col = [pc_ref[0, c * unroll + ci, :, lanes[p]].T[:, 0:1] for ci, p in units]

        lhs = [jnp.concatenate([at[u], rp[u]], axis=0) for u in us]
        rhs = [jnp.concatenate([stack(bm[u]), stack(km[u])], axis=0) for u in us]
        amat = [lax.dot_general(lhs[u], rhs[u], _NT, preferred_element_type=F32) for u in us]
        a_ab = [jnp.where(strict, m[0:c_len, 0:LANES], 0.0) for m in amat]
        a_ak = [jnp.where(strict, m[0:c_len, LANES:2 * LANES], 0.0) for m in amat]
        a_rb = [jnp.where(incl, m[c_len:, 0:LANES], 0.0) for m in amat]
        a_rk = [jnp.where(incl, m[c_len:, LANES:2 * LANES], 0.0) for m in amat]

        pw = a_ab
        t_inv = [eye_cat + x for x in a_ab]
        pw_bd = [stack(x) for x in pw]
        pw = [_bdot(pw[u], pw_bd[u]) for u in us]
        for _ in range(int(math.log2(c_len)) - 2):
            pw_bd = [stack(x) for x in pw]
            both = [_bdot(jnp.concatenate([pw[u], t_inv[u]], axis=0), pw_bd[u]) for u in us]
            pw = [x[0:c_len] for x in both]
            t_inv = [t_inv[u] + both[u][c_len:] for u in us]
        pw_bd = [stack(x) for x in pw]
        t_inv = [t_inv[u] + _bdot(t_inv[u], pw_bd[u]) for u in us]

        v_st = [stack(x) for x in v]
        akv = [_bdot(a_ak[u], v_st[u]) for u in us]
        sol = [_bdot(t_inv[u], jnp.concatenate([stack(at[u]), stack(akv[u])], axis=1)) for u in us]
        y0 = [_bdot(a_rk[u], v_st[u]) for u in us]
        hkv = [jnp.where(bd_mask, _bdot(kc_t[u], v[u]), 0.0) for u in us]

        h = [h_ref[p] for p in prs]
        ys = []
        for ci in range(unroll):
            un = [ci * pairs + p for p in prs]
            hb = [x.astype(BF16) for x in h]
            uu = [jnp.dot(sol[un[p]][:, 0:LANES].astype(BF16), hb[p], preferred_element_type=F32)
                  + sol[un[p]][:, LANES:] for p in prs]
            y = [jnp.dot(rp[un[p]].astype(BF16), hb[p], preferred_element_type=F32) + y0[un[p]] for p in prs]
            ys += [y[p] + _bdot(a_rb[un[p]], stack(uu[p])) for p in prs]
            h = [pc_col[un[p]] * h[p] + jnp.where(bd_mask, _bdot(bc_t[un[p]], uu[p]), 0.0) + hkv[un[p]]
                 for p in prs]
        for p in prs:
            h_ref[p] = h[p]

        mean = [_split_dot(x, ones_bd) * (1.0 / hd) for x in ys]
        dlt = [ys[u] - mean[u] for u in us]
        var = [_split_dot(x * x, ones_bd) * (1.0 / hd) for x in dlt]
        for u, (ci, p) in enumerate(units):
            yn = dlt[u] * lax.rsqrt(var[u] + GN_EPS) * gng[p] + gnb[p]
            out = (yn + bonus_ref[0, rows[ci], lanes[p]]) * g_ref[0, rows[ci], lanes[p]]
            o_ref[0, rows[ci], lanes[p]] = out.astype(o_ref.dtype)
        return carry

    lax.fori_loop(0, chunks // unroll, chunk_step, 0)


def _rwkv_scan(prep, gn_g, gn_b, cast_jobs, *, ts):
    b, s, bw = prep[0].shape
    assert ts % (RWKV_CHUNK * RWKV_UNROLL) == 0 and s % ts == 0 and bw % LANES == 0
    pairs = bw // LANES
    tok = pl.BlockSpec((1, ts, bw), lambda bi, t: (bi, t, 0))
    pcs = pl.BlockSpec((1, ts // RWKV_CHUNK, SUBLANES, bw), lambda bi, t: (bi, t, 0, 0))
    vec = pl.BlockSpec((1, bw), lambda bi, t: (0, 0))
    nt = s // ts
    c_in, c_out, c_shape, c_args = _cast_specs(cast_jobs, b * nt, lambda bi, t: bi * nt + t)
    outs = pl.pallas_call(
        functools.partial(_scan_body, ncast=len(cast_jobs), chunks=ts // RWKV_CHUNK, pairs=pairs,
                          unroll=RWKV_UNROLL),
        grid=(b, nt),
        in_specs=[tok] * 7 + [pcs, tok, tok, vec, vec] + c_in,
        out_specs=[tok] + c_out,
        out_shape=[jax.ShapeDtypeStruct((b, s, bw), BF16)] + c_shape,
        scratch_shapes=[pltpu.VMEM((pairs, LANES, LANES), F32)],
        compiler_params=_params("parallel", "arbitrary"),
        name="rwkv_scan",
    )(*prep, gn_g, gn_b, *c_args)
    return outs[0], outs[1:]


def _outproj_body(h_ref, ya_ref, yb_ref, wa_ref, wb_ref, o_ref):
    acc = jnp.dot(ya_ref[...].astype(BF16), wa_ref[...], preferred_element_type=F32)
    acc = acc + jnp.dot(yb_ref[...].astype(BF16), wb_ref[...], preferred_element_type=F32)
    o_ref[...] = h_ref[...] + acc


def _outproj(h, ya, yb, w_out, e, *, tm, tn):
    t, d = h.shape
    wa, wb = ya.shape[-1], yb.shape[-1]
    assert wa == wb
    return pl.pallas_call(
        _outproj_body,
        grid=(t // tm, d // tn),
        in_specs=[
            pl.BlockSpec((tm, tn), lambda i, j: (i, j)),
            pl.BlockSpec((tm, wa), lambda i, j: (i, 0)),
            pl.BlockSpec((tm, wb), lambda i, j: (i, 0)),
            pl.BlockSpec((None, wa, tn), lambda i, j: (e, 0, j)),
            pl.BlockSpec((None, wb, tn), lambda i, j: (e, 1, j)),
        ],
        out_specs=pl.BlockSpec((tm, tn), lambda i, j: (i, j)),
        out_shape=jax.ShapeDtypeStruct((t, d), F32),
        compiler_params=_params("parallel", "arbitrary"),
        name="outproj",
    )(h, ya, yb, w_out, w_out)


def _pool_body(h_ref, halo_ref, g_ref, w_ref, sc_ref, o_ref, *, windows, halo):
    ts, d = h_ref.shape[1], h_ref.shape[2]
    grp = d // len(windows)
    t = pl.program_id(1)
    hcur = h_ref[0]
    u = _rms(hcur, g_ref[...])
    uh = jnp.where(t == 0, 0.0, _rms(halo_ref[0], g_ref[...]))
    pos = t * ts + lax.broadcasted_iota(jnp.int32, (ts, 1), 0)
    outs = []
    for gi, win in enumerate(windows):
        cols = slice(gi * grp, (gi + 1) * grp)
        ug = u[:, cols]
        ext = jnp.concatenate([uh[:, cols], ug], axis=0)
        span = 1
        while span < win:
            ext = ext + pltpu.roll(ext, span, axis=0)
            span *= 2
        cnt = jnp.minimum(pos + 1, win).astype(F32)
        diff = ext[halo:, :] / cnt - ug
        outs.append(jnp.dot(diff.astype(BF16), w_ref[gi], preferred_element_type=F32))
    o_ref[0] = hcur + jnp.concatenate(outs, axis=-1) * sc_ref[...]


def _pool(h, gain, w_grp, o, scale, *, ts):
    b, s, d = h.shape
    halo = 2 * SUBLANES
    assert max(POOL_WINDOWS) <= halo and all(w & (w - 1) == 0 for w in POOL_WINDOWS)
    assert s % ts == 0 and ts % halo == 0
    ng, grp = w_grp.shape[1], w_grp.shape[2]
    sub = ts // halo
    return pl.pallas_call(
        functools.partial(_pool_body, windows=POOL_WINDOWS, halo=halo),
        grid=(b, s // ts),
        in_specs=[
            pl.BlockSpec((1, ts, d), lambda bi, t: (bi, t, 0)),
            pl.BlockSpec((1, halo, d), lambda bi, t: (bi, jnp.maximum(t * sub - 1, 0), 0)),
            pl.BlockSpec((1, d), lambda bi, t: (0, 0)),
            pl.BlockSpec((None, ng, grp, grp), lambda bi, t: (o, 0, 0, 0)),
            pl.BlockSpec((1, d), lambda bi, t: (0, 0)),
        ],
        out_specs=pl.BlockSpec((1, ts, d), lambda bi, t: (bi, t, 0)),
        out_shape=jax.ShapeDtypeStruct((b, s, d), F32),
        compiler_params=_params("parallel", "arbitrary"),
        name="pool",
    )(h, h, gain, w_grp, scale)


def _pick(n, pref):
    for c in range(min(pref, n), 0, -1):
        if n % c == 0 and (c % SUBLANES == 0 or c == n):
            return c
    return n


def kernel(x, ffn1_norm, ffn1_wg, ffn1_wu, ffn1_wd, mix_norm, ffn2_norm, ffn2_wg, ffn2_wu, ffn2_wd,
           ab_w_in, ab_w_out, rwkv_mu, rwkv_w0, rwkv_w2, rwkv_a0, rwkv_a2, rwkv_g2, rwkv_k_k, rwkv_k_a,
           rwkv_r_k, rwkv_gn_g, rwkv_gn_b, pool_w, pool_scale, final_norm):
    b, s, d = x.shape
    t = b * s
    depth = ffn1_norm.shape[0]
    bw = rwkv_w0.shape[-1]
    aw = ab_w_out.shape[1] - bw
    a_heads = aw // A_HEAD_DIM
    zb_col = 3 * aw
    assert rwkv_w2.shape[1] == LORA and 3 * LORA <= 2 * LANES

    f_ff = ffn1_wg.shape[-1]
    ffn_w = [(ffn1_wg, ffn1_wu, ffn1_wd), (ffn2_wg, ffn2_wu, ffn2_wd)]
    n_ffn = 2 * depth
    ffn_b = {0: tuple(w[0].astype(BF16) for w in ffn_w[0])}

    def cast_jobs(ns):
        return [(w.reshape(depth, d, f_ff), n // 2) for n in ns for w in ffn_w[n % 2]]

    def casted(ns, outs):
        for k, n in enumerate(ns):
            g_, u_, d_ = outs[3 * k:3 * k + 3]
            ffn_b[n] = (g_, u_, d_.reshape(f_ff, d))
    n_in = ab_w_in.shape[-1]
    n_pad = zb_col + 3 * bw + 2 * LANES - n_in
    w_in = jnp.pad(ab_w_in, ((0, 0), (0, 0), (0, n_pad))).astype(BF16)
    w_out = ab_w_out.astype(BF16)
    pool_wb = pool_w.astype(BF16)
    zrow = jnp.zeros((LANES - LORA, bw), F32)

    tm = _pick(t, TOKEN_TILE)
    tmo = _pick(t, OUTPROJ_TOKEN_TILE)
    tf = _pick(ffn1_wg.shape[-1], FF_TILE)
    row = lambda v: v.reshape(1, -1)

    h = x.reshape(t, d)
    for layer in range(depth):
        h = _ffn(h, row(ffn1_norm[layer]), *ffn_b[2 * layer], row(final_norm), final=False, tm=tm, tf=tf)
        if layer % 2 == 0:
            e = layer // 2
            z = _inproj(h, row(mix_norm[layer]), w_in, e, tm=tm, tn=_pick(w_in.shape[-1], INPROJ_COL_TILE))
            z = z.reshape(b, s, -1)
            todo = list(range(2 * layer + 1, min(2 * layer + 5, n_ffn)))
            ns_a, ns_b = todo[:(len(todo) + 1) // 2], todo[(len(todo) + 1) // 2:]
            ya, c_a = _moba(z, cast_jobs(ns_a), heads=a_heads, hg=math.gcd(a_heads, MOBA_HEAD_GROUP))
            casted(ns_a, c_a)
            mu = rwkv_mu[e]
            prm = {
                "mu_r": row(mu[0:bw]), "mu_k": row(mu[bw:2 * bw]), "mu_v": row(mu[2 * bw:3 * bw]),
                "mu_l": row(jnp.pad(mu[3 * bw:], (0, n_pad))),
                "w0": row(rwkv_w0[e]), "a0": row(rwkv_a0[e]), "k_k": row(rwkv_k_k[e]),
                "k_a": row(rwkv_k_a[e]), "r_k": row(rwkv_r_k[e]),
                "w2": jnp.concatenate([rwkv_w2[e], zrow], axis=0).astype(BF16),
                "a2": jnp.concatenate([zrow, rwkv_a2[e]], axis=0).astype(BF16),
                "g2": jnp.concatenate([rwkv_g2[e], zrow], axis=0).astype(BF16),
            }
            prep = _rwkv_prep(z, prm, zb_col=zb_col, ts=_pick(s, RWKV_PREP_TILE))
            yb, c_b = _rwkv_scan(prep, row(rwkv_gn_g[e]), row(rwkv_gn_b[e]), cast_jobs(ns_b),
                                 ts=_pick(s, RWKV_SCAN_TILE))
            casted(ns_b, c_b)
            h = _outproj(h, ya.reshape(t, aw), yb.reshape(t, bw), w_out, e, tm=tmo, tn=d)
        else:
            o = layer // 2
            h = _pool(h.reshape(b, s, d), row(mix_norm[layer]), pool_wb, o, row(pool_scale[o]),
                      ts=_pick(s, POOL_TILE)).reshape(t, d)
        h = _ffn(h, row(ffn2_norm[layer]), *ffn_b[2 * layer + 1], row(final_norm),
                 final=(layer == depth - 1), tm=tm, tf=tf)
    return h.reshape(b, s, d)
```

```python
import functools
import math

import jax
import jax.numpy as jnp
from jax import lax
from jax.experimental import pallas as pl
from jax.experimental.pallas import tpu as pltpu

F32 = jnp.float32
BF16 = jnp.bfloat16

A_HEAD_DIM = 128
MOBA_BLOCK = 256
MOBA_TOPK = 3
B_HEAD_DIM = 64
LORA = 64
GN_EPS = 64e-5
RMS_EPS = 1e-6
POOL_WINDOWS = (2, 4, 8, 16)

LANES = 128
SUBLANES = 8
VMEM_LIMIT_BYTES = 56 * 1024 * 1024
FFN_VMEM_LIMIT_BYTES = 58 * 1024 * 1024

LOG2E = math.log2(math.e)
NEG_BIG = -1e30
MOBA_HEAD_GROUP = 4
RWKV_UNROLL = 2
RWKV_CHUNK = 64

TOKEN_TILE = 1024
FF_TILE = 512
INPROJ_COL_TILE = 1280
OUTPROJ_TOKEN_TILE = 512
RWKV_PREP_TILE = 512
RWKV_SCAN_TILE = 512
POOL_TILE = 512

_NT = (((1,), (1,)), ((), ()))


def _rms(x, gain):
    ms = jnp.mean(x * x, axis=-1, keepdims=True)
    return x * lax.rsqrt(ms + RMS_EPS) * gain


def _params(*sem, vmem=VMEM_LIMIT_BYTES):
    return pltpu.CompilerParams(dimension_semantics=sem, vmem_limit_bytes=vmem)


def _ffn_body(h_ref, g_ref, wg_ref, wu_ref, wd_ref, fg_ref, o_ref, n_ref, *, final):
    f = pl.program_id(1)

    @pl.when(f == 0)
    def _():
        h = h_ref[...]
        n_ref[...] = _rms(h, g_ref[...]).astype(BF16)
        o_ref[...] = h

    n = n_ref[...]
    gate = jnp.dot(n, wg_ref[...], preferred_element_type=F32)
    up = jnp.dot(n, wu_ref[...], preferred_element_type=F32)
    act = (gate * jax.nn.sigmoid(gate) * up).astype(BF16)
    o_ref[...] += 0.5 * jnp.dot(act, wd_ref[...], preferred_element_type=F32)

    if final:
        @pl.when(f == pl.num_programs(1) - 1)
        def _():
            o_ref[...] = _rms(o_ref[...], fg_ref[...])


def _ffn(h, gain, wg, wu, wd, final_gain, *, final, tm, tf):
    t, d = h.shape
    f = wg.shape[-1]
    return pl.pallas_call(
        functools.partial(_ffn_body, final=final),
        grid=(t // tm, f // tf),
        in_specs=[
            pl.BlockSpec((tm, d), lambda i, j: (i, 0)),
            pl.BlockSpec((1, d), lambda i, j: (0, 0)),
            pl.BlockSpec((d, tf), lambda i, j: (0, j)),
            pl.BlockSpec((d, tf), lambda i, j: (0, j)),
            pl.BlockSpec((tf, d), lambda i, j: (j, 0)),
            pl.BlockSpec((1, d), lambda i, j: (0, 0)),
        ],
        out_specs=pl.BlockSpec((tm, d), lambda i, j: (i, 0)),
        out_shape=jax.ShapeDtypeStruct((t, d), F32),
        scratch_shapes=[pltpu.VMEM((tm, d), BF16)],
        compiler_params=_params("parallel", "arbitrary", vmem=FFN_VMEM_LIMIT_BYTES),
        name="ffn",
    )(h, gain, wg, wu, wd, final_gain)


def _cast_specs(jobs, steps, flat):
    in_specs, out_specs, out_shapes, args = [], [], [], []
    for w, layer in jobs:
        _, r, c = w.shape
        rows = r // steps
        assert r % steps == 0 and rows % (2 * SUBLANES) == 0
        in_specs.append(pl.BlockSpec((None, rows, c), lambda *ids, layer=layer: (layer, flat(*ids), 0)))
        out_specs.append(pl.BlockSpec((rows, c), lambda *ids: (flat(*ids), 0)))
        out_shapes.append(jax.ShapeDtypeStruct((r, c), BF16))
        args.append(w)
    return in_specs, out_specs, out_shapes, args


def _inproj_body(h_ref, g_ref, w_ref, z_ref, n_ref):
    @pl.when(pl.program_id(1) == 0)
    def _():
        n_ref[...] = _rms(h_ref[...], g_ref[...]).astype(BF16)

    z_ref[...] = jnp.dot(n_ref[...], w_ref[...], preferred_element_type=F32)


def _inproj(h, gain, w_in, e, *, tm, tn):
    t, d = h.shape
    n = w_in.shape[-1]
    return pl.pallas_call(
        _inproj_body,
        grid=(t // tm, n // tn),
        in_specs=[
            pl.BlockSpec((tm, d), lambda i, j: (i, 0)),
            pl.BlockSpec((1, d), lambda i, j: (0, 0)),
            pl.BlockSpec((None, d, tn), lambda i, j: (e, 0, j)),
        ],
        out_specs=pl.BlockSpec((tm, tn), lambda i, j: (i, j)),
        out_shape=jax.ShapeDtypeStruct((t, n), F32),
        scratch_shapes=[pltpu.VMEM((tm, d), BF16)],
        compiler_params=_params("parallel", "arbitrary"),
        name="inproj",
    )(h, gain, w_in)


def _split_bf16(x, parts):
    out = []
    for _ in range(parts):
        p = x.astype(BF16)
        out.append(p)
        x = x - p.astype(F32)
    return out


def _nt_dot_3pass(a, b):
    ah, al = _split_bf16(a, 2)
    bh, bl = _split_bf16(b, 2)
    n = a.shape[0]
    both = lax.dot_general(jnp.concatenate([ah, al], axis=0), bh, _NT, preferred_element_type=F32)
    return both[0:n] + both[n:] + lax.dot_general(ah, bl, _NT, preferred_element_type=F32)


def _moba_body(q_ref, k_ref, v_ref, *rest, ncast, hg, nb, blk, topk, scale):
    cast_in, o_ref, cast_out = rest[:ncast], rest[ncast], rest[ncast + 1:2 * ncast + 1]
    kb_ref, vt_ref, km_ref, sel_ref, s0_ref, s1_ref, acc_ref = rest[2 * ncast + 1:]
    for src, dst in zip(cast_in, cast_out):
        dst[...] = src[...].astype(BF16)
    i = pl.program_id(2)
    dh = A_HEAD_DIM
    cols = [slice(hh * dh, (hh + 1) * dh) for hh in range(hg)]

    @pl.when(i == 0)
    def _():
        km_ref[...] = jnp.zeros_like(km_ref)
        for hh in range(hg):
            for jb in range(nb):
                rows = slice(jb * blk, (jb + 1) * blk)
                kf = k_ref[0, rows, cols[hh]]
                kb_ref[hh, jb] = kf.astype(BF16)
                vt_ref[hh, jb] = v_ref[0, rows, cols[hh]].T.astype(BF16)
                km_ref[hh, jb:jb + 1, :] = jnp.mean(kf, axis=0, keepdims=True)

    qs = [q_ref[0, :, c] for c in cols]
    for hh in range(hg):
        gate = _nt_dot_3pass(km_ref[hh], qs[hh])
        blk_id = lax.broadcasted_iota(jnp.int32, gate.shape, 0)
        rank = jnp.zeros(gate.shape, jnp.int32)
        for jp in range(nb):
            row = gate[jp:jp + 1, :]
            beats = (row > gate) | ((row == gate) & (jp < blk_id))
            rank = rank + jnp.where(beats & (jp < i), 1, 0)
        sel_ref[hh] = jnp.where((blk_id < i) & (rank < topk), 0.0, NEG_BIG)

    qb = [(q * (scale * LOG2E)).astype(BF16) for q in qs]

    def scores(hh, j):
        return lax.dot_general(kb_ref[hh, j], qb[hh], _NT, preferred_element_type=F32)

    def pv(hh, p, j):
        return jnp.dot(vt_ref[hh, j], p.astype(BF16), preferred_element_type=F32)

    kpos = lax.broadcasted_iota(jnp.int32, (blk, blk), 0)
    qpos = lax.broadcasted_iota(jnp.int32, (blk, blk), 1)
    causal = kpos <= qpos
    heads = range(hg)
    ss = [jnp.where(causal, scores(hh, i), NEG_BIG) for hh in heads]
    ms = [jnp.max(s, axis=0, keepdims=True) for s in ss]
    ps = [jnp.exp2(s - m) for s, m in zip(ss, ms)]
    ls = [jnp.sum(p, axis=0, keepdims=True) for p in ps]
    accs = [pv(hh, ps[hh], i) for hh in heads]

    for hh in heads:
        acc_ref[hh] = accs[hh]

    def produce(j, s_ref):
        mx = []
        for hh in heads:
            s = scores(hh, jnp.minimum(j, nb - 1))
            s_ref[hh] = s
            mx.append(jnp.max(s, axis=0, keepdims=True))
        return tuple(mx)

    def consume(j, s_ref, ms, ls, mx):
        sel = [sel_ref[hh, pl.ds(j, 1), :] for hh in heads]
        m_new = [jnp.maximum(m, x + b) for m, x, b in zip(ms, mx, sel)]
        alpha = [jnp.exp2(m - mn) for m, mn in zip(ms, m_new)]
        shift = [mn - 2.0 * b for mn, b in zip(m_new, sel)]
        ps = [jnp.exp2(s_ref[hh] - shift[hh]) for hh in heads]
        ls = [a * l + jnp.sum(p, axis=0, keepdims=True) for a, l, p in zip(alpha, ls, ps)]
        pvs = [pv(hh, ps[hh], j) for hh in heads]
        for hh in heads:
            acc_ref[hh] = alpha[hh] * acc_ref[hh] + pvs[hh]
        return tuple(m_new), tuple(ls)

    def past(jj, carry):
        ms, ls, mx0 = carry
        j0 = 2 * jj
        mx1 = produce(j0 + 1, s1_ref)
        ms, ls = consume(j0, s0_ref, ms, ls, mx0)
        mx0 = produce(j0 + 2, s0_ref)
        ms, ls = consume(j0 + 1, s1_ref, ms, ls, mx1)
        return ms, ls, mx0

    mx0 = produce(0, s0_ref)
    _, ls, _ = lax.fori_loop(0, (i + 1) // 2, past, (tuple(ms), tuple(ls), mx0))
    for hh in heads:
        o_ref[0, :, cols[hh]] = (acc_ref[hh] / ls[hh]).T.astype(o_ref.dtype)


def _moba(z, cast_jobs, *, heads, hg):
    b, s, _ = z.shape
    dh, blk = A_HEAD_DIM, MOBA_BLOCK
    nb = s // blk
    nbp = -(-nb // SUBLANES) * SUBLANES
    ng = heads // hg
    assert s % blk == 0 and heads % hg == 0
    c_in, c_out, c_shape, c_args = _cast_specs(cast_jobs, b * ng * nb, lambda bi, g, i: (bi * ng + g) * nb + i)
    outs = pl.pallas_call(
        functools.partial(_moba_body, ncast=len(cast_jobs), hg=hg, nb=nb, blk=blk, topk=min(MOBA_TOPK, nb),
                          scale=dh ** -0.5),
        grid=(b, ng, nb),
        in_specs=[
            pl.BlockSpec((1, blk, hg * dh), lambda bi, g, i: (bi, i, g)),
            pl.BlockSpec((1, s, hg * dh), lambda bi, g, i: (bi, 0, ng + g)),
            pl.BlockSpec((1, s, hg * dh), lambda bi, g, i: (bi, 0, 2 * ng + g)),
        ] + c_in,
        out_specs=[pl.BlockSpec((1, blk, hg * dh), lambda bi, g, i: (bi, i, g))] + c_out,
        out_shape=[jax.ShapeDtypeStruct((b, s, heads * dh), BF16)] + c_shape,
        scratch_shapes=[pltpu.VMEM((hg, nb, blk, dh), BF16), pltpu.VMEM((hg, nb, dh, blk), BF16),
                        pltpu.VMEM((hg, nbp, dh), F32), pltpu.VMEM((hg, nbp, blk), F32),
                        pltpu.VMEM((hg, blk, blk), F32), pltpu.VMEM((hg, blk, blk), F32),
                        pltpu.VMEM((hg, dh, blk), F32)],
        compiler_params=_params("parallel", "parallel", "arbitrary"),
        name="moba",
    )(z, z, z, *c_args)
    return outs[0], outs[1:]


def _split_dot(x, ones_bd):
    hi = x.astype(BF16)
    lo = (x - hi.astype(F32)).astype(BF16)
    return (jnp.dot(hi, ones_bd, preferred_element_type=F32)
            + jnp.dot(lo, ones_bd, preferred_element_type=F32))


def _head_sum(x, ones_bd):
    cols = [_split_dot(x[:, c:c + LANES], ones_bd) for c in range(0, x.shape[-1], LANES)]
    return cols[0] if len(cols) == 1 else jnp.concatenate(cols, axis=-1)


def _head_ones():
    r = lax.broadcasted_iota(jnp.int32, (LANES, LANES), 0) // B_HEAD_DIM
    c = lax.broadcasted_iota(jnp.int32, (LANES, LANES), 1) // B_HEAD_DIM
    return jnp.where(r == c, 1.0, 0.0).astype(BF16)


def _prep_body(zr_ref, zk_ref, zv_ref, zl_ref, pr_ref, pk_ref, pv_ref, plo_ref,
               mur_ref, muk_ref, muv_ref, mul_ref, w0_ref, a0_ref, kk_ref, ka_ref, rk_ref,
               w2_ref, a2_ref, g2_ref, tri_ref, ones_ref,
               at_out, rp_out, bm_out, km_out, bc_out, kc_out, v_out, pc_out, bonus_out, g_out):
    ts = zr_ref.shape[1]
    first = pl.program_id(1) == 0
    row8 = lax.broadcasted_iota(jnp.int32, (SUBLANES, 1), 0)

    def lerp(z_ref, prev_ref, mu_ref):
        x = z_ref[0]
        prev = jnp.where(first, 0.0, prev_ref[0, SUBLANES - 1:SUBLANES, :])
        rolled = pltpu.roll(x, 1, axis=0)
        head = jnp.where(row8 == 0, prev, rolled[0:SUBLANES])
        shifted = jnp.concatenate([head, rolled[SUBLANES:]], axis=0)
        return x + (shifted - x) * mu_ref[...]

    r = lerp(zr_ref, pr_ref, mur_ref)
    k = lerp(zk_ref, pk_ref, muk_ref)
    v = lerp(zv_ref, pv_ref, muv_ref)
    lo = lerp(zl_ref, plo_ref, mul_ref)
    wa = lo[:, 0:LANES]
    gl = lo[:, LANES:2 * LANES]

    w_in = w0_ref[...] + jnp.dot(jnp.tanh(wa).astype(BF16), w2_ref[...], preferred_element_type=F32)
    lw = -math.exp(-0.5) * jax.nn.sigmoid(w_in)
    a = jax.nn.sigmoid(a0_ref[...] + jnp.dot(wa.astype(BF16), a2_ref[...], preferred_element_type=F32))
    g = jnp.dot(jax.nn.sigmoid(gl).astype(BF16), g2_ref[...], preferred_element_type=F32)

    ones_bd = ones_ref[...]
    kk = k * kk_ref[...]
    kk = kk * lax.rsqrt(jnp.maximum(_head_sum(kk * kk, ones_bd), 1e-24))
    k2 = k * (1.0 + (a - 1.0) * ka_ref[...])
    bonus = _head_sum(r * k2 * rk_ref[...], ones_bd) * v

    c_len = RWKV_CHUNK
    nc = ts // c_len
    bw = lw.shape[1]
    lw2 = lw * LOG2E
    y = jnp.dot(tri_ref[...], jnp.concatenate(_split_bf16(lw2, 3), axis=1), preferred_element_type=F32)
    cum = y[:, 0:bw] + y[:, bw:2 * bw] + y[:, 2 * bw:]
    last = cum.reshape(nc, c_len, bw)[:, c_len - 1:c_len, :]
    total = jnp.broadcast_to(last, (nc, c_len, bw)).reshape(ts, bw)
    p_inv = jnp.exp2(-cum)
    to_end = jnp.exp2(total - cum)
    b = kk * a
    at_out[0] = (-kk * jnp.exp2(cum - lw2)).astype(BF16)
    rp_out[0] = (r * jnp.exp2(cum)).astype(BF16)
    bm_out[0] = (b * p_inv).astype(BF16)
    km_out[0] = (k2 * p_inv).astype(BF16)
    bc_out[0] = (b * to_end).astype(BF16)
    kc_out[0] = (k2 * to_end).astype(BF16)
    v_out[0] = v.astype(BF16)
    pc_out[0] = jnp.exp2(jnp.broadcast_to(last, (nc, SUBLANES, bw)))
    bonus_out[0] = bonus
    g_out[0] = g


def _rwkv_prep(z, p, *, zb_col, ts):
    b, s, _ = z.shape
    bw = p["w0"].shape[-1]
    lw = 2 * LANES
    cb = zb_col // bw
    lb = (zb_col + 3 * bw) // lw
    assert zb_col % bw == 0 and (zb_col + 3 * bw) % lw == 0 and s % ts == 0 and ts % SUBLANES == 0
    sub = ts // SUBLANES

    def cur(col, width):
        return pl.BlockSpec((1, ts, width), lambda bi, t: (bi, t, col))

    def prev(col, width):
        return pl.BlockSpec((1, SUBLANES, width), lambda bi, t: (bi, jnp.maximum(t * sub - 1, 0), col))

    def vec(width):
        return pl.BlockSpec((1, width), lambda bi, t: (0, 0))

    def mat(width):
        return pl.BlockSpec((LANES, width), lambda bi, t: (0, 0))

    assert ts % RWKV_CHUNK == 0
    nc = ts // RWKV_CHUNK
    ti = jnp.arange(ts)
    tri = ((ti[:, None] // RWKV_CHUNK == ti[None, :] // RWKV_CHUNK) & (ti[None, :] <= ti[:, None])).astype(BF16)
    li = jnp.arange(LANES)
    ones_bd = (li[:, None] // B_HEAD_DIM == li[None, :] // B_HEAD_DIM).astype(BF16)

    def whole(shape):
        return pl.BlockSpec(shape, lambda bi, t: (0, 0))

    out_spec = pl.BlockSpec((1, ts, bw), lambda bi, t: (bi, t, 0))
    pc_spec = pl.BlockSpec((1, nc, SUBLANES, bw), lambda bi, t: (bi, t, 0, 0))
    tok16 = jax.ShapeDtypeStruct((b, s, bw), BF16)
    tok32 = jax.ShapeDtypeStruct((b, s, bw), F32)
    pc_shape = jax.ShapeDtypeStruct((b, s // RWKV_CHUNK, SUBLANES, bw), F32)
    return pl.pallas_call(
        _prep_body,
        grid=(b, s // ts),
        in_specs=[cur(cb, bw), cur(cb + 1, bw), cur(cb + 2, bw), cur(lb, lw),
                  prev(cb, bw), prev(cb + 1, bw), prev(cb + 2, bw), prev(lb, lw),
                  vec(bw), vec(bw), vec(bw), vec(lw), vec(bw), vec(bw), vec(bw), vec(bw), vec(bw),
                  mat(bw), mat(bw), mat(bw),
                  whole((ts, ts)), whole((LANES, LANES))],
        out_specs=[out_spec] * 7 + [pc_spec, out_spec, out_spec],
        out_shape=[tok16] * 7 + [pc_shape, tok32, tok32],
        compiler_params=_params("parallel", "arbitrary"),
        name="rwkv_prep",
    )(z, z, z, z, z, z, z, z,
      p["mu_r"], p["mu_k"], p["mu_v"], p["mu_l"], p["w0"], p["a0"], p["k_k"], p["k_a"], p["r_k"],
      p["w2"], p["a2"], p["g2"], tri, ones_bd)


def _bdot(a, b):
    return jnp.dot(a.astype(BF16), b.astype(BF16), preferred_element_type=F32)


def _scan_body(at_ref, rp_ref, bm_ref, km_ref, bc_ref, kc_ref, v_ref, pc_ref, bonus_ref, g_ref, gng_ref, gnb_ref,
               *rest, ncast, chunks, pairs, unroll):
    cast_in, o_ref, cast_out, h_ref = rest[:ncast], rest[ncast], rest[ncast + 1:2 * ncast + 1], rest[2 * ncast + 1]
    for src, dst in zip(cast_in, cast_out):
        dst[...] = src[...].astype(BF16)
    c_len = RWKV_CHUNK
    hd = B_HEAD_DIM
    prs = range(pairs)
    lanes = [slice(p * LANES, (p + 1) * LANES) for p in prs]

    @pl.when(pl.program_id(1) == 0)
    def _():
        h_ref[...] = jnp.zeros_like(h_ref)

    lane = lax.broadcasted_iota(jnp.int32, (c_len, LANES), 1)
    rowc = lax.broadcasted_iota(jnp.int32, (c_len, LANES), 0)
    head0 = lane < hd
    scol = lane % hd
    strict = scol < rowc
    incl = scol <= rowc
    r2 = lax.broadcasted_iota(jnp.int32, (LANES, LANES), 0)
    c2 = lax.broadcasted_iota(jnp.int32, (LANES, LANES), 1)
    bd_mask = (r2 // hd) == (c2 // hd)
    eye_cat = jnp.where(scol == rowc, 1.0, 0.0)
    ones_bd = _head_ones()
    gng = [gng_ref[:, ln] for ln in lanes]
    gnb = [gnb_ref[:, ln] for ln in lanes]

    def stack(x):
        return jnp.concatenate([jnp.where(head0, x, 0.0), jnp.where(head0, 0.0, x)], axis=0)

    def chunk_step(c, carry):
        rows = [pl.ds(pl.multiple_of((c * unroll + ci) * c_len, c_len), c_len) for ci in range(unroll)]
        units = [(ci, p) for ci in range(unroll) for p in prs]
        us = range(len(units))

        def load(ref):
            return [ref[0, rows[ci], lanes[p]] for ci, p in units]

        at, rp, bm, km, v = load(at_ref), load(rp_ref), load(bm_ref), load(km_ref), load(v_ref)
        bc_t = [x.astype(F32).T for x in load(bc_ref)]
        kc_t = [x.astype(F32).T for x in load(kc_ref)]
        pc_col = [pc_ref[0, c * unroll + ci, :, lanes[p]].T[:, 0:1] for ci, p in units]

        lhs = [jnp.concatenate([at[u], rp[u]], axis=0) for u in us]
        rhs = [jnp.concatenate([stack(bm[u]), stack(km[u])], axis=0) for u in us]
        amat = [lax.dot_general(lhs[u], rhs[u], _NT, preferred_element_type=F32) for u in us]
        a_ab = [jnp.where(strict, m[0:c_len, 0:LANES], 0.0) for m in amat]
        a_ak = [jnp.where(strict, m[0:c_len, LANES:2 * LANES], 0.0) for m in amat]
        a_rb = [jnp.where(incl, m[c_len:, 0:LANES], 0.0) for m in amat]
        a_rk = [jnp.where(incl, m[c_len:, LANES:2 * LANES], 0.0) for m in amat]

        pw = a_ab
        t_inv = [eye_cat + x for x in a_ab]
        pw_bd = [stack(x) for x in pw]
        pw = [_bdot(pw[u], pw_bd[u]) for u in us]
        for _ in range(int(math.log2(c_len)) - 2):
            pw_bd = [stack(x) for x in pw]
            both = [_bdot(jnp.concatenate([pw[u], t_inv[u]], axis=0), pw_bd[u]) for u in us]
            pw = [x[0:c_len] for x in both]
            t_inv = [t_inv[u] + both[u][c_len:] for u in us]
        pw_bd = [stack(x) for x in pw]
        t_inv = [t_inv[u] + _bdot(t_inv[u], pw_bd[u]) for u in us]

        v_st = [stack(x) for x in v]
        akv = [_bdot(a_ak[u], v_st[u]) for u in us]
        sol = [_bdot(t_inv[u], jnp.concatenate([stack(at[u]), stack(akv[u])], axis=1)) for u in us]
        y0 = [_bdot(a_rk[u], v_st[u]) for u in us]
        hkv = [jnp.where(bd_mask, _bdot(kc_t[u], v[u]), 0.0) for u in us]

        h = [h_ref[p] for p in prs]
        ys = []
        for ci in range(unroll):
            un = [ci * pairs + p for p in prs]
            hb = [x.astype(BF16) for x in h]
            uu = [jnp.dot(sol[un[p]][:, 0:LANES].astype(BF16), hb[p], preferred_element_type=F32)
                  + sol[un[p]][:, LANES:] for p in prs]
            y = [jnp.dot(rp[un[p]].astype(BF16), hb[p], preferred_element_type=F32) + y0[un[p]] for p in prs]
            ys += [y[p] + _bdot(a_rb[un[p]], stack(uu[p])) for p in prs]
            h = [pc_col[un[p]] * h[p] + jnp.where(bd_mask, _bdot(bc_t[un[p]], uu[p]), 0.0) + hkv[un[p]]
                 for p in prs]
        for p in prs:
            h_ref[p] = h[p]

        mean = [_split_dot(x, ones_bd) * (1.0 / hd) for x in ys]
        dlt = [ys[u] - mean[u] for u in us]
        var = [_split_dot(x * x, ones_bd) * (1.0 / hd) for x in dlt]
        for u, (ci, p) in enumerate(units):
            yn = dlt[u] * lax.rsqrt(var[u] + GN_EPS) * gng[p] + gnb[p]
            out = (yn + bonus_ref[0, rows[ci], lanes[p]]) * g_ref[0, rows[ci], lanes[p]]
            o_ref[0, rows[ci], lanes[p]] = out.astype(o_ref.dtype)
        return carry

    lax.fori_loop(0, chunks // unroll, chunk_step, 0)


def _rwkv_scan(prep, gn_g, gn_b, cast_jobs, *, ts):
    b, s, bw = prep[0].shape
    assert ts % (RWKV_CHUNK * RWKV_UNROLL) == 0 and s % ts == 0 and bw % LANES == 0
    pairs = bw // LANES
    tok = pl.BlockSpec((1, ts, bw), lambda bi, t: (bi, t, 0))
    pcs = pl.BlockSpec((1, ts // RWKV_CHUNK, SUBLANES, bw), lambda bi, t: (bi, t, 0, 0))
    vec = pl.BlockSpec((1, bw), lambda bi, t: (0, 0))
    nt = s // ts
    c_in, c_out, c_shape, c_args = _cast_specs(cast_jobs, b * nt, lambda bi, t: bi * nt + t)
    outs = pl.pallas_call(
        functools.partial(_scan_body, ncast=len(cast_jobs), chunks=ts // RWKV_CHUNK, pairs=pairs,
                          unroll=RWKV_UNROLL),
        grid=(b, nt),
        in_specs=[tok] * 7 + [pcs, tok, tok, vec, vec] + c_in,
        out_specs=[tok] + c_out,
        out_shape=[jax.ShapeDtypeStruct((b, s, bw), BF16)] + c_shape,
        scratch_shapes=[pltpu.VMEM((pairs, LANES, LANES), F32)],
        compiler_params=_params("parallel", "arbitrary"),
        name="rwkv_scan",
    )(*prep, gn_g, gn_b, *c_args)
    return outs[0], outs[1:]


def _outproj_body(h_ref, ya_ref, yb_ref, wa_ref, wb_ref, o_ref):
    acc = jnp.dot(ya_ref[...].astype(BF16), wa_ref[...], preferred_element_type=F32)
    acc = acc + jnp.dot(yb_ref[...].astype(BF16), wb_ref[...], preferred_element_type=F32)
    o_ref[...] = h_ref[...] + acc


def _outproj(h, ya, yb, w_out, e, *, tm, tn):
    t, d = h.shape
    wa, wb = ya.shape[-1], yb.shape[-1]
    assert wa == wb
    return pl.pallas_call(
        _outproj_body,
        grid=(t // tm, d // tn),
        in_specs=[
            pl.BlockSpec((tm, tn), lambda i, j: (i, j)),
            pl.BlockSpec((tm, wa), lambda i, j: (i, 0)),
            pl.BlockSpec((tm, wb), lambda i, j: (i, 0)),
            pl.BlockSpec((None, wa, tn), lambda i, j: (e, 0, j)),
            pl.BlockSpec((None, wb, tn), lambda i, j: (e, 1, j)),
        ],
        out_specs=pl.BlockSpec((tm, tn), lambda i, j: (i, j)),
        out_shape=jax.ShapeDtypeStruct((t, d), F32),
        compiler_params=_params("parallel", "arbitrary"),
        name="outproj",
    )(h, ya, yb, w_out, w_out)


def _pool_body(h_ref, halo_ref, g_ref, w_ref, sc_ref, o_ref, *, windows, halo):
    ts, d = h_ref.shape[1], h_ref.shape[2]
    grp = d // len(windows)
    t = pl.program_id(1)
    hcur = h_ref[0]
    u = _rms(hcur, g_ref[...])
    uh = jnp.where(t == 0, 0.0, _rms(halo_ref[0], g_ref[...]))
    pos = t * ts + lax.broadcasted_iota(jnp.int32, (ts, 1), 0)
    outs = []
    for gi, win in enumerate(windows):
        cols = slice(gi * grp, (gi + 1) * grp)
        ug = u[:, cols]
        ext = jnp.concatenate([uh[:, cols], ug], axis=0)
        span = 1
        while span < win:
            ext = ext + pltpu.roll(ext, span, axis=0)
            span *= 2
        cnt = jnp.minimum(pos + 1, win).astype(F32)
        diff = ext[halo:, :] / cnt - ug
        outs.append(jnp.dot(diff.astype(BF16), w_ref[gi], preferred_element_type=F32))
    o_ref[0] = hcur + jnp.concatenate(outs, axis=-1) * sc_ref[...]


def _pool(h, gain, w_grp, o, scale, *, ts):
    b, s, d = h.shape
    halo = 2 * SUBLANES
    assert max(POOL_WINDOWS) <= halo and all(w & (w - 1) == 0 for w in POOL_WINDOWS)
    assert s % ts == 0 and ts % halo == 0
    ng, grp = w_grp.shape[1], w_grp.shape[2]
    sub = ts // halo
    return pl.pallas_call(
        functools.partial(_pool_body, windows=POOL_WINDOWS, halo=halo),
        grid=(b, s // ts),
        in_specs=[
            pl.BlockSpec((1, ts, d), lambda bi, t: (bi, t, 0)),
            pl.BlockSpec((1, halo, d), lambda bi, t: (bi, jnp.maximum(t * sub - 1, 0), 0)),
            pl.BlockSpec((1, d), lambda bi, t: (0, 0)),
            pl.BlockSpec((None, ng, grp, grp), lambda bi, t: (o, 0, 0, 0)),
            pl.BlockSpec((1, d), lambda bi, t: (0, 0)),
        ],
        out_specs=pl.BlockSpec((1, ts, d), lambda bi, t: (bi, t, 0)),
        out_shape=jax.ShapeDtypeStruct((b, s, d), F32),
        compiler_params=_params("parallel", "arbitrary"),
        name="pool",
    )(h, h, gain, w_grp, scale)


def _pick(n, pref):
    for c in range(min(pref, n), 0, -1):
        if n % c == 0 and (c % SUBLANES == 0 or c == n):
            return c
    return n


def kernel(x, ffn1_norm, ffn1_wg, ffn1_wu, ffn1_wd, mix_norm, ffn2_norm, ffn2_wg, ffn2_wu, ffn2_wd,
           ab_w_in, ab_w_out, rwkv_mu, rwkv_w0, rwkv_w2, rwkv_a0, rwkv_a2, rwkv_g2, rwkv_k_k, rwkv_k_a,
           rwkv_r_k, rwkv_gn_g, rwkv_gn_b, pool_w, pool_scale, final_norm):
    b, s, d = x.shape
    t = b * s
    depth = ffn1_norm.shape[0]
    bw = rwkv_w0.shape[-1]
    aw = ab_w_out.shape[1] - bw
    a_heads = aw // A_HEAD_DIM
    zb_col = 3 * aw
    assert rwkv_w2.shape[1] == LORA and 3 * LORA <= 2 * LANES

    ffn_w = [(ffn1_wg, ffn1_wu, ffn1_wd), (ffn2_wg, ffn2_wu, ffn2_wd)]
    n_ffn = 2 * depth
    ffn_b = {n: [None] * 3 for n in range(n_ffn)}
    ffn_b[0] = [w[0].astype(BF16) for w in ffn_w[0]]

    def cast_jobs(items):
        return [(ffn_w[n % 2][k], n // 2) for n, k in items]

    def casted(items, outs):
        for (n, k), w in zip(items, outs):
            ffn_b[n][k] = w

    n_in = ab_w_in.shape[-1]
    n_pad = zb_col + 3 * bw + 2 * LANES - n_in
    w_in = jnp.pad(ab_w_in, ((0, 0), (0, 0), (0, n_pad))).astype(BF16)
    w_out = ab_w_out.astype(BF16)
    pool_wb = pool_w.astype(BF16)
    zrow = jnp.zeros((LANES - LORA, bw), F32)

    tm = _pick(t, TOKEN_TILE)
    tmo = _pick(t, OUTPROJ_TOKEN_TILE)
    tf = _pick(ffn1_wg.shape[-1], FF_TILE)
    row = lambda v: v.reshape(1, -1)

    h = x.reshape(t, d)
    for layer in range(depth):
        h = _ffn(h, row(ffn1_norm[layer]), *ffn_b[2 * layer], row(final_norm), final=False, tm=tm, tf=tf)
        if layer % 2 == 0:
            e = layer // 2
            z = _inproj(h, row(mix_norm[layer]), w_in, e, tm=tm, tn=_pick(w_in.shape[-1], INPROJ_COL_TILE))
            z = z.reshape(b, s, -1)
            todo = list(range(2 * layer + 1, min(2 * layer + 5, n_ffn)))
            ns_a = [(n, k) for n in todo[:3] for k in (0, 1)]
            ns_b = [(n, k) for n in todo[3:] for k in (0, 1)] + [(n, 2) for n in todo]
            ya, c_a = _moba(z, cast_jobs(ns_a), heads=a_heads, hg=math.gcd(a_heads, MOBA_HEAD_GROUP))
            casted(ns_a, c_a)
            mu = rwkv_mu[e]
            prm = {
                "mu_r": row(mu[0:bw]), "mu_k": row(mu[bw:2 * bw]), "mu_v": row(mu[2 * bw:3 * bw]),
                "mu_l": row(jnp.pad(mu[3 * bw:], (0, n_pad))),
                "w0": row(rwkv_w0[e]), "a0": row(rwkv_a0[e]), "k_k": row(rwkv_k_k[e]),
                "k_a": row(rwkv_k_a[e]), "r_k": row(rwkv_r_k[e]),
                "w2": jnp.concatenate([rwkv_w2[e], zrow], axis=0).astype(BF16),
                "a2": jnp.concatenate([zrow, rwkv_a2[e]], axis=0).astype(BF16),
                "g2": jnp.concatenate([rwkv_g2[e], zrow], axis=0).astype(BF16),
            }
            prep = _rwkv_prep(z, prm, zb_col=zb_col, ts=_pick(s, RWKV_PREP_TILE))
            yb, c_b = _rwkv_scan(prep, row(rwkv_gn_g[e]), row(rwkv_gn_b[e]), cast_jobs(ns_b),
                                 ts=_pick(s, RWKV_SCAN_TILE))
            casted(ns_b, c_b)
            h = _outproj(h, ya.reshape(t, aw), yb.reshape(t, bw), w_out, e, tm=tmo, tn=d)
        else:
            o = layer // 2
            h = _pool(h.reshape(b, s, d), row(mix_norm[layer]), pool_wb, o, row(pool_scale[o]),
                      ts=_pick(s, POOL_TILE)).reshape(t, d)
        h = _ffn(h, row(ffn2_norm[layer]), *ffn_b[2 * layer + 1], row(final_norm),
                 final=(layer == depth - 1), tm=tm, tf=tf)
    return h.reshape(b, s, d)
```

```python
import functools
import math

import jax
import jax.numpy as jnp
from jax import lax
from jax.experimental import pallas as pl
from jax.experimental.pallas import tpu as pltpu

F32 = jnp.float32
BF16 = jnp.bfloat16

A_HEAD_DIM = 128
MOBA_BLOCK = 256
MOBA_TOPK = 3
B_HEAD_DIM = 64
LORA = 64
GN_EPS = 64e-5
RMS_EPS = 1e-6
POOL_WINDOWS = (2, 4, 8, 16)

LANES = 128
SUBLANES = 8
VMEM_LIMIT_BYTES = 56 * 1024 * 1024
FFN_VMEM_LIMIT_BYTES = 58 * 1024 * 1024

LOG2E = math.log2(math.e)
NEG_BIG = -1e30
MOBA_HEAD_GROUP = 4
RWKV_UNROLL = 2
RWKV_CHUNK = 64

TOKEN_TILE = 1024
FF_TILE = 512
INPROJ_COL_TILE = 1280
OUTPROJ_TOKEN_TILE = 512
RWKV_PREP_TILE = 512
RWKV_SCAN_TILE = 512
POOL_TILE = 512
CAST_ROWS = 256

_NT = (((1,), (1,)), ((), ()))


def _rms(x, gain):
    ms = jnp.mean(x * x, axis=-1, keepdims=True)
    return x * lax.rsqrt(ms + RMS_EPS) * gain


def _params(*sem, vmem=VMEM_LIMIT_BYTES):
    return pltpu.CompilerParams(dimension_semantics=sem, vmem_limit_bytes=vmem)


def _ffn_body(h_ref, g_ref, wg_ref, wu_ref, wd_ref, fg_ref, o_ref, n_ref, *, final):
    f = pl.program_id(1)

    @pl.when(f == 0)
    def _():
        h = h_ref[...]
        n_ref[...] = _rms(h, g_ref[...]).astype(BF16)
        o_ref[...] = h

    n = n_ref[...]
    gate = jnp.dot(n, wg_ref[...], preferred_element_type=F32)
    up = jnp.dot(n, wu_ref[...], preferred_element_type=F32)
    act = (gate * jax.nn.sigmoid(gate) * up).astype(BF16)
    o_ref[...] += 0.5 * jnp.dot(act, wd_ref[...], preferred_element_type=F32)

    if final:
        @pl.when(f == pl.num_programs(1) - 1)
        def _():
            o_ref[...] = _rms(o_ref[...], fg_ref[...])


def _ffn(h, gain, wg, wu, wd, final_gain, *, final, tm, tf):
    t, d = h.shape
    f = wg.shape[-1]
    return pl.pallas_call(
        functools.partial(_ffn_body, final=final),
        grid=(t // tm, f // tf),
        in_specs=[
            pl.BlockSpec((tm, d), lambda i, j: (i, 0)),
            pl.BlockSpec((1, d), lambda i, j: (0, 0)),
            pl.BlockSpec((d, tf), lambda i, j: (0, j)),
            pl.BlockSpec((d, tf), lambda i, j: (0, j)),
            pl.BlockSpec((tf, d), lambda i, j: (j, 0)),
            pl.BlockSpec((1, d), lambda i, j: (0, 0)),
        ],
        out_specs=pl.BlockSpec((tm, d), lambda i, j: (i, 0)),
        out_shape=jax.ShapeDtypeStruct((t, d), F32),
        scratch_shapes=[pltpu.VMEM((tm, d), BF16)],
        compiler_params=_params("parallel", "arbitrary", vmem=FFN_VMEM_LIMIT_BYTES),
        name="ffn",
    )(h, gain, wg, wu, wd, final_gain)


def _cast_specs(jobs, steps, flat):
    in_specs, out_specs, out_shapes, args = [], [], [], []
    for w, layer in jobs:
        _, r, c = w.shape
        rows = r // steps
        assert r % steps == 0 and rows % (2 * SUBLANES) == 0
        in_specs.append(pl.BlockSpec((None, rows, c), lambda *ids, layer=layer: (layer, flat(*ids), 0)))
        out_specs.append(pl.BlockSpec((rows, c), lambda *ids: (flat(*ids), 0)))
        out_shapes.append(jax.ShapeDtypeStruct((r, c), BF16))
        args.append(w)
    return in_specs, out_specs, out_shapes, args


def _cast_body(w_ref, o_ref):
    c = w_ref.shape[-1]
    if o_ref.shape[-1] == c:
        o_ref[...] = w_ref[...].astype(BF16)
    else:
        o_ref[:, 0:c] = w_ref[...].astype(BF16)
        o_ref[:, c:] = jnp.zeros((o_ref.shape[0], o_ref.shape[-1] - c), BF16)


def _cast_matrix(w, layer, *, rows, pad=0):
    _, r, c = w.shape
    assert r % rows == 0 and rows % (2 * SUBLANES) == 0
    return pl.pallas_call(
        _cast_body,
        grid=(r // rows,),
        in_specs=[pl.BlockSpec((None, rows, c), lambda i: (layer, i, 0))],
        out_specs=pl.BlockSpec((rows, c + pad), lambda i: (i, 0)),
        out_shape=jax.ShapeDtypeStruct((r, c + pad), BF16),
        compiler_params=_params("parallel"),
        name="cast",
    )(w)


def _inproj_body(h_ref, g_ref, w_ref, z_ref, n_ref):
    @pl.when(pl.program_id(1) == 0)
    def _():
        n_ref[...] = _rms(h_ref[...], g_ref[...]).astype(BF16)

    z_ref[...] = jnp.dot(n_ref[...], w_ref[...], preferred_element_type=F32)


def _inproj(h, gain, w_in, *, tm, tn):
    t, d = h.shape
    n = w_in.shape[-1]
    return pl.pallas_call(
        _inproj_body,
        grid=(t // tm, n // tn),
        in_specs=[
            pl.BlockSpec((tm, d), lambda i, j: (i, 0)),
            pl.BlockSpec((1, d), lambda i, j: (0, 0)),
            pl.BlockSpec((d, tn), lambda i, j: (0, j)),
        ],
        out_specs=pl.BlockSpec((tm, tn), lambda i, j: (i, j)),
        out_shape=jax.ShapeDtypeStruct((t, n), F32),
        scratch_shapes=[pltpu.VMEM((tm, d), BF16)],
        compiler_params=_params("parallel", "arbitrary"),
        name="inproj",
    )(h, gain, w_in)


def _split_bf16(x, parts):
    out = []
    for _ in range(parts):
        p = x.astype(BF16)
        out.append(p)
        x = x - p.astype(F32)
    return out


def _nt_dot_3pass(a, b):
    ah, al = _split_bf16(a, 2)
    bh, bl = _split_bf16(b, 2)
    n = a.shape[0]
    both = lax.dot_general(jnp.concatenate([ah, al], axis=0), bh, _NT, preferred_element_type=F32)
    return both[0:n] + both[n:] + lax.dot_general(ah, bl, _NT, preferred_element_type=F32)


def _moba_body(q_ref, k_ref, v_ref, *rest, ncast, hg, nb, blk, topk, scale):
    cast_in, o_ref, cast_out = rest[:ncast], rest[ncast], rest[ncast + 1:2 * ncast + 1]
    kb_ref, vt_ref, km_ref, sel_ref, s0_ref, s1_ref, acc_ref = rest[2 * ncast + 1:]
    for src, dst in zip(cast_in, cast_out):
        dst[...] = src[...].astype(BF16)
    i = pl.program_id(2)
    dh = A_HEAD_DIM
    cols = [slice(hh * dh, (hh + 1) * dh) for hh in range(hg)]

    @pl.when(i == 0)
    def _():
        km_ref[...] = jnp.zeros_like(km_ref)
        for hh in range(hg):
            for jb in range(nb):
                rows = slice(jb * blk, (jb + 1) * blk)
                kf = k_ref[0, rows, cols[hh]]
                kb_ref[hh, jb] = kf.astype(BF16)
                vt_ref[hh, jb] = v_ref[0, rows, cols[hh]].T.astype(BF16)
                km_ref[hh, jb:jb + 1, :] = jnp.mean(kf, axis=0, keepdims=True)

    qs = [q_ref[0, :, c] for c in cols]
    for hh in range(hg):
        gate = _nt_dot_3pass(km_ref[hh], qs[hh])
        blk_id = lax.broadcasted_iota(jnp.int32, gate.shape, 0)
        rank = jnp.zeros(gate.shape, jnp.int32)
        for jp in range(nb):
            row = gate[jp:jp + 1, :]
            beats = (row > gate) | ((row == gate) & (jp < blk_id))
            rank = rank + jnp.where(beats & (jp < i), 1, 0)
        sel_ref[hh] = jnp.where((blk_id < i) & (rank < topk), 0.0, NEG_BIG)

    qb = [(q * (scale * LOG2E)).astype(BF16) for q in qs]

    def scores(hh, j):
        return lax.dot_general(kb_ref[hh, j], qb[hh], _NT, preferred_element_type=F32)

    def pv(hh, p, j):
        return jnp.dot(vt_ref[hh, j], p.astype(BF16), preferred_element_type=F32)

    kpos = lax.broadcasted_iota(jnp.int32, (blk, blk), 0)
    qpos = lax.broadcasted_iota(jnp.int32, (blk, blk), 1)
    causal = kpos <= qpos
    heads = range(hg)
    ss = [jnp.where(causal, scores(hh, i), NEG_BIG) for hh in heads]
    ms = [jnp.max(s, axis=0, keepdims=True) for s in ss]
    ps = [jnp.exp2(s - m) for s, m in zip(ss, ms)]
    ls = [jnp.sum(p, axis=0, keepdims=True) for p in ps]
    accs = [pv(hh, ps[hh], i) for hh in heads]

    for hh in heads:
        acc_ref[hh] = accs[hh]

    def produce(j, s_ref):
        mx = []
        for hh in heads:
            s = scores(hh, jnp.minimum(j, nb - 1))
            s_ref[hh] = s
            mx.append(jnp.max(s, axis=0, keepdims=True))
        return tuple(mx)

    def consume(j, s_ref, ms, ls, mx):
        sel = [sel_ref[hh, pl.ds(j, 1), :] for hh in heads]
        m_new = [jnp.maximum(m, x + b) for m, x, b in zip(ms, mx, sel)]
        alpha = [jnp.exp2(m - mn) for m, mn in zip(ms, m_new)]
        shift = [mn - 2.0 * b for mn, b in zip(m_new, sel)]
        ps = [jnp.exp2(s_ref[hh] - shift[hh]) for hh in heads]
        ls = [a * l + jnp.sum(p, axis=0, keepdims=True) for a, l, p in zip(alpha, ls, ps)]
        pvs = [pv(hh, ps[hh], j) for hh in heads]
        for hh in heads:
            acc_ref[hh] = alpha[hh] * acc_ref[hh] + pvs[hh]
        return tuple(m_new), tuple(ls)

    def past(jj, carry):
        ms, ls, mx0 = carry
        j0 = 2 * jj
        mx1 = produce(j0 + 1, s1_ref)
        ms, ls = consume(j0, s0_ref, ms, ls, mx0)
        mx0 = produce(j0 + 2, s0_ref)
        ms, ls = consume(j0 + 1, s1_ref, ms, ls, mx1)
        return ms, ls, mx0

    mx0 = produce(0, s0_ref)
    _, ls, _ = lax.fori_loop(0, (i + 1) // 2, past, (tuple(ms), tuple(ls), mx0))
    for hh in heads:
        o_ref[0, :, cols[hh]] = (acc_ref[hh] / ls[hh]).T.astype(o_ref.dtype)


def _moba(z, cast_jobs, *, heads, hg):
    b, s, _ = z.shape
    dh, blk = A_HEAD_DIM, MOBA_BLOCK
    nb = s // blk
    nbp = -(-nb // SUBLANES) * SUBLANES
    ng = heads // hg
    assert s % blk == 0 and heads % hg == 0
    c_in, c_out, c_shape, c_args = _cast_specs(cast_jobs, b * ng * nb, lambda bi, g, i: (bi * ng + g) * nb + i)
    outs = pl.pallas_call(
        functools.partial(_moba_body, ncast=len(cast_jobs), hg=hg, nb=nb, blk=blk, topk=min(MOBA_TOPK, nb),
                          scale=dh ** -0.5),
        grid=(b, ng, nb),
        in_specs=[
            pl.BlockSpec((1, blk, hg * dh), lambda bi, g, i: (bi, i, g)),
            pl.BlockSpec((1, s, hg * dh), lambda bi, g, i: (bi, 0, ng + g)),
            pl.BlockSpec((1, s, hg * dh), lambda bi, g, i: (bi, 0, 2 * ng + g)),
        ] + c_in,
        out_specs=[pl.BlockSpec((1, blk, hg * dh), lambda bi, g, i: (bi, i, g))] + c_out,
        out_shape=[jax.ShapeDtypeStruct((b, s, heads * dh), BF16)] + c_shape,
        scratch_shapes=[pltpu.VMEM((hg, nb, blk, dh), BF16), pltpu.VMEM((hg, nb, dh, blk), BF16),
                        pltpu.VMEM((hg, nbp, dh), F32), pltpu.VMEM((hg, nbp, blk), F32),
                        pltpu.VMEM((hg, blk, blk), F32), pltpu.VMEM((hg, blk, blk), F32),
                        pltpu.VMEM((hg, dh, blk), F32)],
        compiler_params=_params("parallel", "parallel", "arbitrary"),
        name="moba",
    )(z, z, z, *c_args)
    return outs[0], outs[1:]


def _split_dot(x, ones_bd):
    hi = x.astype(BF16)
    lo = (x - hi.astype(F32)).astype(BF16)
    return (jnp.dot(hi, ones_bd, preferred_element_type=F32)
            + jnp.dot(lo, ones_bd, preferred_element_type=F32))


def _head_sum(x, ones_bd):
    cols = [_split_dot(x[:, c:c + LANES], ones_bd) for c in range(0, x.shape[-1], LANES)]
    return cols[0] if len(cols) == 1 else jnp.concatenate(cols, axis=-1)


def _head_ones():
    r = lax.broadcasted_iota(jnp.int32, (LANES, LANES), 0) // B_HEAD_DIM
    c = lax.broadcasted_iota(jnp.int32, (LANES, LANES), 1) // B_HEAD_DIM
    return jnp.where(r == c, 1.0, 0.0).astype(BF16)


def _prep_body(zr_ref, zk_ref, zv_ref, zl_ref, pr_ref, pk_ref, pv_ref, plo_ref,
               mur_ref, muk_ref, muv_ref, mul_ref, w0_ref, a0_ref, kk_ref, ka_ref, rk_ref,
               w2_ref, a2_ref, g2_ref, tri_ref, ones_ref,
               at_out, rp_out, bm_out, km_out, bc_out, kc_out, v_out, pc_out, bonus_out, g_out):
    ts = zr_ref.shape[1]
    first = pl.program_id(1) == 0
    row8 = lax.broadcasted_iota(jnp.int32, (SUBLANES, 1), 0)

    def lerp(z_ref, prev_ref, mu_ref):
        x = z_ref[0]
        prev = jnp.where(first, 0.0, prev_ref[0, SUBLANES - 1:SUBLANES, :])
        rolled = pltpu.roll(x, 1, axis=0)
        head = jnp.where(row8 == 0, prev, rolled[0:SUBLANES])
        shifted = jnp.concatenate([head, rolled[SUBLANES:]], axis=0)
        return x + (shifted - x) * mu_ref[...]

    r = lerp(zr_ref, pr_ref, mur_ref)
    k = lerp(zk_ref, pk_ref, muk_ref)
    v = lerp(zv_ref, pv_ref, muv_ref)
    lo = lerp(zl_ref, plo_ref, mul_ref)
    wa = lo[:, 0:LANES]
    gl = lo[:, LANES:2 * LANES]

    w_in = w0_ref[...] + jnp.dot(jnp.tanh(wa).astype(BF16), w2_ref[...], preferred_element_type=F32)
    lw = -math.exp(-0.5) * jax.nn.sigmoid(w_in)
    a = jax.nn.sigmoid(a0_ref[...] + jnp.dot(wa.astype(BF16), a2_ref[...], preferred_element_type=F32))
    g = jnp.dot(jax.nn.sigmoid(gl).astype(BF16), g2_ref[...], preferred_element_type=F32)

    ones_bd = ones_ref[...]
    kk = k * kk_ref[...]
    kk = kk * lax.rsqrt(jnp.maximum(_head_sum(kk * kk, ones_bd), 1e-24))
    k2 = k * (1.0 + (a - 1.0) * ka_ref[...])
    bonus = _head_sum(r * k2 * rk_ref[...], ones_bd) * v

    c_len = RWKV_CHUNK
    nc = ts // c_len
    bw = lw.shape[1]
    lw2 = lw * LOG2E
    y = jnp.dot(tri_ref[...], jnp.concatenate(_split_bf16(lw2, 3), axis=1), preferred_element_type=F32)
    cum = y[:, 0:bw] + y[:, bw:2 * bw] + y[:, 2 * bw:]
    last = cum.reshape(nc, c_len, bw)[:, c_len - 1:c_len, :]
    total = jnp.broadcast_to(last, (nc, c_len, bw)).reshape(ts, bw)
    p_inv = jnp.exp2(-cum)
    to_end = jnp.exp2(total - cum)
    b = kk * a
    at_out[0] = (-kk * jnp.exp2(cum - lw2)).astype(BF16)
    rp_out[0] = (r * jnp.exp2(cum)).astype(BF16)
    bm_out[0] = (b * p_inv).astype(BF16)
    km_out[0] = (k2 * p_inv).astype(BF16)
    bc_out[0] = (b * to_end).astype(BF16)
    kc_out[0] = (k2 * to_end).astype(BF16)
    v_out[0] = v.astype(BF16)
    pc_out[0] = jnp.exp2(jnp.broadcast_to(last, (nc, SUBLANES, bw)))
    bonus_out[0] = bonus
    g_out[0] = g


def _rwkv_prep(z, p, *, zb_col, ts):
    b, s, _ = z.shape
    bw = p["w0"].shape[-1]
    lw = 2 * LANES
    cb = zb_col // bw
    lb = (zb_col + 3 * bw) // lw
    assert zb_col % bw == 0 and (zb_col + 3 * bw) % lw == 0 and s % ts == 0 and ts % SUBLANES == 0
    sub = ts // SUBLANES

    def cur(col, width):
        return pl.BlockSpec((1, ts, width), lambda bi, t: (bi, t, col))

    def prev(col, width):
        return pl.BlockSpec((1, SUBLANES, width), lambda bi, t: (bi, jnp.maximum(t * sub - 1, 0), col))

    def vec(width):
        return pl.BlockSpec((1, width), lambda bi, t: (0, 0))

    def mat(width):
        return pl.BlockSpec((LANES, width), lambda bi, t: (0, 0))

    assert ts % RWKV_CHUNK == 0
    nc = ts // RWKV_CHUNK
    ti = jnp.arange(ts)
    tri = ((ti[:, None] // RWKV_CHUNK == ti[None, :] // RWKV_CHUNK) & (ti[None, :] <= ti[:, None])).astype(BF16)
    li = jnp.arange(LANES)
    ones_bd = (li[:, None] // B_HEAD_DIM == li[None, :] // B_HEAD_DIM).astype(BF16)

    def whole(shape):
        return pl.BlockSpec(shape, lambda bi, t: (0, 0))

    out_spec = pl.BlockSpec((1, ts, bw), lambda bi, t: (bi, t, 0))
    pc_spec = pl.BlockSpec((1, nc, SUBLANES, bw), lambda bi, t: (bi, t, 0, 0))
    tok16 = jax.ShapeDtypeStruct((b, s, bw), BF16)
    tok32 = jax.ShapeDtypeStruct((b, s, bw), F32)
    pc_shape = jax.ShapeDtypeStruct((b, s // RWKV_CHUNK, SUBLANES, bw), F32)
    return pl.pallas_call(
        _prep_body,
        grid=(b, s // ts),
        in_specs=[cur(cb, bw), cur(cb + 1, bw), cur(cb + 2, bw), cur(lb, lw),
                  prev(cb, bw), prev(cb + 1, bw), prev(cb + 2, bw), prev(lb, lw),
                  vec(bw), vec(bw), vec(bw), vec(lw), vec(bw), vec(bw), vec(bw), vec(bw), vec(bw),
                  mat(bw), mat(bw), mat(bw),
                  whole((ts, ts)), whole((LANES, LANES))],
        out_specs=[out_spec] * 7 + [pc_spec, out_spec, out_spec],
        out_shape=[tok16] * 7 + [pc_shape, tok32, tok32],
        compiler_params=_params("parallel", "arbitrary"),
        name="rwkv_prep",
    )(z, z, z, z, z, z, z, z,
      p["mu_r"], p["mu_k"], p["mu_v"], p["mu_l"], p["w0"], p["a0"], p["k_k"], p["k_a"], p["r_k"],
      p["w2"], p["a2"], p["g2"], tri, ones_bd)


def _bdot(a, b):
    return jnp.dot(a.astype(BF16), b.astype(BF16), preferred_element_type=F32)


def _scan_body(at_ref, rp_ref, bm_ref, km_ref, bc_ref, kc_ref, v_ref, pc_ref, bonus_ref, g_ref, gng_ref, gnb_ref,
               *rest, ncast, chunks, pairs, unroll):
    cast_in, o_ref, cast_out, h_ref = rest[:ncast], rest[ncast], rest[ncast + 1:2 * ncast + 1], rest[2 * ncast + 1]
    for src, dst in zip(cast_in, cast_out):
        dst[...] = src[...].astype(BF16)
    c_len = RWKV_CHUNK
    hd = B_HEAD_DIM
    prs = range(pairs)
    lanes = [slice(p * LANES, (p + 1) * LANES) for p in prs]

    @pl.when(pl.program_id(1) == 0)
    def _():
        h_ref[...] = jnp.zeros_like(h_ref)

    lane = lax.broadcasted_iota(jnp.int32, (c_len, LANES), 1)
    rowc = lax.broadcasted_iota(jnp.int32, (c_len, LANES), 0)
    head0 = lane < hd
    scol = lane % hd
    strict = scol < rowc
    incl = scol <= rowc
    r2 = lax.broadcasted_iota(jnp.int32, (LANES, LANES), 0)
    c2 = lax.broadcasted_iota(jnp.int32, (LANES, LANES), 1)
    bd_mask = (r2 // hd) == (c2 // hd)
    eye_cat = jnp.where(scol == rowc, 1.0, 0.0)
    ones_bd = _head_ones()
    gng = [gng_ref[:, ln] for ln in lanes]
    gnb = [gnb_ref[:, ln] for ln in lanes]

    def stack(x):
        return jnp.concatenate([jnp.where(head0, x, 0.0), jnp.where(head0, 0.0, x)], axis=0)

    def chunk_step(c, carry):
        rows = [pl.ds(pl.multiple_of((c * unroll + ci) * c_len, c_len), c_len) for ci in range(unroll)]
        units = [(ci, p) for ci in range(unroll) for p in prs]
        us = range(len(units))

        def load(ref):
            return [ref[0, rows[ci], lanes[p]] for ci, p in units]

        at, rp, bm, km, v = load(at_ref), load(rp_ref), load(bm_ref), load(km_ref), load(v_ref)
        bc_t = [x.astype(F32).T for x in load(bc_ref)]
        kc_t = [x.astype(F32).T for x in load(kc_ref)]
        pc_col = [pc_ref[0, c * unroll + ci, :, lanes[p]].T[:, 0:1] for ci, p in units]

        lhs = [jnp.concatenate([at[u], rp[u]], axis=0) for u in us]
        rhs = [jnp.concatenate([stack(bm[u]), stack(km[u])], axis=0) for u in us]
        amat = [lax.dot_general(lhs[u], rhs[u], _NT, preferred_element_type=F32) for u in us]
        a_ab = [jnp.where(strict, m[0:c_len, 0:LANES], 0.0) for m in amat]
        a_ak = [jnp.where(strict, m[0:c_len, LANES:2 * LANES], 0.0) for m in amat]
        a_rb = [jnp.where(incl, m[c_len:, 0:LANES], 0.0) for m in amat]
        a_rk = [jnp.where(incl, m[c_len:, LANES:2 * LANES], 0.0) for m in amat]

        pw = a_ab
        t_inv = [eye_cat + x for x in a_ab]
        pw_bd = [stack(x) for x in pw]
        pw = [_bdot(pw[u], pw_bd[u]) for u in us]
        for _ in range(int(math.log2(c_len)) - 2):
            pw_bd = [stack(x) for x in pw]
            both = [_bdot(jnp.concatenate([pw[u], t_inv[u]], axis=0), pw_bd[u]) for u in us]
            pw = [x[0:c_len] for x in both]
            t_inv = [t_inv[u] + both[u][c_len:] for u in us]
        pw_bd = [stack(x) for x in pw]
        t_inv = [t_inv[u] + _bdot(t_inv[u], pw_bd[u]) for u in us]

        v_st = [stack(x) for x in v]
        akv = [_bdot(a_ak[u], v_st[u]) for u in us]
        sol = [_bdot(t_inv[u], jnp.concatenate([stack(at[u]), stack(akv[u])], axis=1)) for u in us]
        y0 = [_bdot(a_rk[u], v_st[u]) for u in us]
        hkv = [jnp.where(bd_mask, _bdot(kc_t[u], v[u]), 0.0) for u in us]

        h = [h_ref[p] for p in prs]
        ys = []
        for ci in range(unroll):
            un = [ci * pairs + p for p in prs]
            hb = [x.astype(BF16) for x in h]
            uu = [jnp.dot(sol[un[p]][:, 0:LANES].astype(BF16), hb[p], preferred_element_type=F32)
                  + sol[un[p]][:, LANES:] for p in prs]
            y = [jnp.dot(rp[un[p]].astype(BF16), hb[p], preferred_element_type=F32) + y0[un[p]] for p in prs]
            ys += [y[p] + _bdot(a_rb[un[p]], stack(uu[p])) for p in prs]
            h = [pc_col[un[p]] * h[p] + jnp.where(bd_mask, _bdot(bc_t[un[p]], uu[p]), 0.0) + hkv[un[p]]
                 for p in prs]
        for p in prs:
            h_ref[p] = h[p]

        mean = [_split_dot(x, ones_bd) * (1.0 / hd) for x in ys]
        dlt = [ys[u] - mean[u] for u in us]
        var = [_split_dot(x * x, ones_bd) * (1.0 / hd) for x in dlt]
        for u, (ci, p) in enumerate(units):
            yn = dlt[u] * lax.rsqrt(var[u] + GN_EPS) * gng[p] + gnb[p]
            out = (yn + bonus_ref[0, rows[ci], lanes[p]]) * g_ref[0, rows[ci], lanes[p]]
            o_ref[0, rows[ci], lanes[p]] = out.astype(o_ref.dtype)
        return carry

    lax.fori_loop(0, chunks // unroll, chunk_step, 0)


def _rwkv_scan(prep, gn_g, gn_b, cast_jobs, *, ts):
    b, s, bw = prep[0].shape
    assert ts % (RWKV_CHUNK * RWKV_UNROLL) == 0 and s % ts == 0 and bw % LANES == 0
    pairs = bw // LANES
    tok = pl.BlockSpec((1, ts, bw), lambda bi, t: (bi, t, 0))
    pcs = pl.BlockSpec((1, ts // RWKV_CHUNK, SUBLANES, bw), lambda bi, t: (bi, t, 0, 0))
    vec = pl.BlockSpec((1, bw), lambda bi, t: (0, 0))
    nt = s // ts
    c_in, c_out, c_shape, c_args = _cast_specs(cast_jobs, b * nt, lambda bi, t: bi * nt + t)
    outs = pl.pallas_call(
        functools.partial(_scan_body, ncast=len(cast_jobs), chunks=ts // RWKV_CHUNK, pairs=pairs,
                          unroll=RWKV_UNROLL),
        grid=(b, nt),
        in_specs=[tok] * 7 + [pcs, tok, tok, vec, vec] + c_in,
        out_specs=[tok] + c_out,
        out_shape=[jax.ShapeDtypeStruct((b, s, bw), BF16)] + c_shape,
        scratch_shapes=[pltpu.VMEM((pairs, LANES, LANES), F32)],
        compiler_params=_params("parallel", "arbitrary"),
        name="rwkv_scan",
    )(*prep, gn_g, gn_b, *c_args)
    return outs[0], outs[1:]


def _outproj_body(h_ref, ya_ref, yb_ref, wa_ref, wb_ref, o_ref):
    acc = jnp.dot(ya_ref[...].astype(BF16), wa_ref[...], preferred_element_type=F32)
    acc = acc + jnp.dot(yb_ref[...].astype(BF16), wb_ref[...], preferred_element_type=F32)
    o_ref[...] = h_ref[...] + acc


def _outproj(h, ya, yb, w_out, e, *, tm, tn):
    t, d = h.shape
    wa, wb = ya.shape[-1], yb.shape[-1]
    assert wa == wb
    return pl.pallas_call(
        _outproj_body,
        grid=(t // tm, d // tn),
        in_specs=[
            pl.BlockSpec((tm, tn), lambda i, j: (i, j)),
            pl.BlockSpec((tm, wa), lambda i, j: (i, 0)),
            pl.BlockSpec((tm, wb), lambda i, j: (i, 0)),
            pl.BlockSpec((None, wa, tn), lambda i, j: (e, 0, j)),
            pl.BlockSpec((None, wb, tn), lambda i, j: (e, 1, j)),
        ],
        out_specs=pl.BlockSpec((tm, tn), lambda i, j: (i, j)),
        out_shape=jax.ShapeDtypeStruct((t, d), F32),
        compiler_params=_params("parallel", "arbitrary"),
        name="outproj",
    )(h, ya, yb, w_out, w_out)


def _pool_body(h_ref, halo_ref, g_ref, w_ref, sc_ref, o_ref, *, windows, halo):
    ts, d = h_ref.shape[1], h_ref.shape[2]
    grp = d // len(windows)
    t = pl.program_id(1)
    hcur = h_ref[0]
    u = _rms(hcur, g_ref[...])
    uh = jnp.where(t == 0, 0.0, _rms(halo_ref[0], g_ref[...]))
    pos = t * ts + lax.broadcasted_iota(jnp.int32, (ts, 1), 0)
    outs = []
    for gi, win in enumerate(windows):
        cols = slice(gi * grp, (gi + 1) * grp)
        ug = u[:, cols]
        ext = jnp.concatenate([uh[:, cols], ug], axis=0)
        span = 1
        while span < win:
            ext = ext + pltpu.roll(ext, span, axis=0)
            span *= 2
        cnt = jnp.minimum(pos + 1, win).astype(F32)
        diff = ext[halo:, :] / cnt - ug
        outs.append(jnp.dot(diff.astype(BF16), w_ref[gi], preferred_element_type=F32))
    o_ref[0] = hcur + jnp.concatenate(outs, axis=-1) * sc_ref[...]


def _pool(h, gain, w_grp, o, scale, *, ts):
    b, s, d = h.shape
    halo = 2 * SUBLANES
    assert max(POOL_WINDOWS) <= halo and all(w & (w - 1) == 0 for w in POOL_WINDOWS)
    assert s % ts == 0 and ts % halo == 0
    ng, grp = w_grp.shape[1], w_grp.shape[2]
    sub = ts // halo
    return pl.pallas_call(
        functools.partial(_pool_body, windows=POOL_WINDOWS, halo=halo),
        grid=(b, s // ts),
        in_specs=[
            pl.BlockSpec((1, ts, d), lambda bi, t: (bi, t, 0)),
            pl.BlockSpec((1, halo, d), lambda bi, t: (bi, jnp.maximum(t * sub - 1, 0), 0)),
            pl.BlockSpec((1, d), lambda bi, t: (0, 0)),
            pl.BlockSpec((None, ng, grp, grp), lambda bi, t: (o, 0, 0, 0)),
            pl.BlockSpec((1, d), lambda bi, t: (0, 0)),
        ],
        out_specs=pl.BlockSpec((1, ts, d), lambda bi, t: (bi, t, 0)),
        out_shape=jax.ShapeDtypeStruct((b, s, d), F32),
        compiler_params=_params("parallel", "arbitrary"),
        name="pool",
    )(h, h, gain, w_grp, scale)


def _pick(n, pref):
    for c in range(min(pref, n), 0, -1):
        if n % c == 0 and (c % SUBLANES == 0 or c == n):
            return c
    return n


def kernel(x, ffn1_norm, ffn1_wg, ffn1_wu, ffn1_wd, mix_norm, ffn2_norm, ffn2_wg, ffn2_wu, ffn2_wd,
           ab_w_in, ab_w_out, rwkv_mu, rwkv_w0, rwkv_w2, rwkv_a0, rwkv_a2, rwkv_g2, rwkv_k_k, rwkv_k_a,
           rwkv_r_k, rwkv_gn_g, rwkv_gn_b, pool_w, pool_scale, final_norm):
    b, s, d = x.shape
    t = b * s
    depth = ffn1_norm.shape[0]
    bw = rwkv_w0.shape[-1]
    aw = ab_w_out.shape[1] - bw
    a_heads = aw // A_HEAD_DIM
    zb_col = 3 * aw
    assert rwkv_w2.shape[1] == LORA and 3 * LORA <= 2 * LANES

    ffn_w = [(ffn1_wg, ffn1_wu, ffn1_wd), (ffn2_wg, ffn2_wu, ffn2_wd)]
    n_ffn = 2 * depth
    ffn_b = {n: [None] * 3 for n in range(n_ffn)}
    ffn_b[0] = [_cast_matrix(w, 0, rows=CAST_ROWS) for w in ffn_w[0]]

    def cast_jobs(items):
        return [(ffn_w[n % 2][k], n // 2) for n, k in items]

    def casted(items, outs):
        for (n, k), w in zip(items, outs):
            ffn_b[n][k] = w

    n_in = ab_w_in.shape[-1]
    n_pad = zb_col + 3 * bw + 2 * LANES - n_in
    w_in = [_cast_matrix(ab_w_in, e, rows=CAST_ROWS, pad=n_pad) for e in range(ab_w_in.shape[0])]
    w_out = ab_w_out.astype(BF16)
    pool_wb = pool_w.astype(BF16)
    zrow = jnp.zeros((LANES - LORA, bw), F32)

    tm = _pick(t, TOKEN_TILE)
    tmo = _pick(t, OUTPROJ_TOKEN_TILE)
    tf = _pick(ffn1_wg.shape[-1], FF_TILE)
    row = lambda v: v.reshape(1, -1)

    h = x.reshape(t, d)
    for layer in range(depth):
        h = _ffn(h, row(ffn1_norm[layer]), *ffn_b[2 * layer], row(final_norm), final=False, tm=tm, tf=tf)
        if layer % 2 == 0:
            e = layer // 2
            z = _inproj(h, row(mix_norm[layer]), w_in[e], tm=tm, tn=_pick(w_in[e].shape[-1], INPROJ_COL_TILE))
            z = z.reshape(b, s, -1)
            todo = list(range(2 * layer + 1, min(2 * layer + 5, n_ffn)))
            ns_a = [(n, k) for n in todo[:3] for k in (0, 1)]
            ns_b = [(n, k) for n in todo[3:] for k in (0, 1)] + [(n, 2) for n in todo]
            ya, c_a = _moba(z, cast_jobs(ns_a), heads=a_heads, hg=math.gcd(a_heads, MOBA_HEAD_GROUP))
            casted(ns_a, c_a)
            mu = rwkv_mu[e]
            prm = {
                "mu_r": row(mu[0:bw]), "mu_k": row(mu[bw:2 * bw]), "mu_v": row(mu[2 * bw:3 * bw]),
                "mu_l": row(jnp.pad(mu[3 * bw:], (0, n_pad))),
                "w0": row(rwkv_w0[e]), "a0": row(rwkv_a0[e]), "k_k": row(rwkv_k_k[e]),
                "k_a": row(rwkv_k_a[e]), "r_k": row(rwkv_r_k[e]),
                "w2": jnp.concatenate([rwkv_w2[e], zrow], axis=0).astype(BF16),
                "a2": jnp.concatenate([zrow, rwkv_a2[e]], axis=0).astype(BF16),
                "g2": jnp.concatenate([rwkv_g2[e], zrow], axis=0).astype(BF16),
            }
            prep = _rwkv_prep(z, prm, zb_col=zb_col, ts=_pick(s, RWKV_PREP_TILE))
            yb, c_b = _rwkv_scan(prep, row(rwkv_gn_g[e]), row(rwkv_gn_b[e]), cast_jobs(ns_b),
                                 ts=_pick(s, RWKV_SCAN_TILE))
            casted(ns_b, c_b)
            h = _outproj(h, ya.reshape(t, aw), yb.reshape(t, bw), w_out, e, tm=tmo, tn=d)
        else:
            o = layer // 2
            h = _pool(h.reshape(b, s, d), row(mix_norm[layer]), pool_wb, o, row(pool_scale[o]),
                      ts=_pick(s, POOL_TILE)).reshape(t, d)
        h = _ffn(h, row(ffn2_norm[layer]), *ffn_b[2 * layer + 1], row(final_norm),
                 final=(layer == depth - 1), tm=tm, tf=tf)
    return h.reshape(b, s, d)
```

```python
import functools
import math

import jax
import jax.numpy as jnp
from jax import lax
from jax.experimental import pallas as pl
from jax.experimental.pallas import tpu as pltpu

F32 = jnp.float32
BF16 = jnp.bfloat16

A_HEAD_DIM = 128
MOBA_BLOCK = 256
MOBA_TOPK = 3
B_HEAD_DIM = 64
LORA = 64
GN_EPS = 64e-5
RMS_EPS = 1e-6
POOL_WINDOWS = (2, 4, 8, 16)

LANES = 128
SUBLANES = 8
VMEM_LIMIT_BYTES = 56 * 1024 * 1024
FFN_VMEM_LIMIT_BYTES = 58 * 1024 * 1024

LOG2E = math.log2(math.e)
NEG_BIG = -1e30
MOBA_HEAD_GROUP = 4
RWKV_UNROLL = 2
RWKV_CHUNK = 64

TOKEN_TILE = 1024
FF_TILE = 512
INPROJ_COL_TILE = 1280
OUTPROJ_TOKEN_TILE = 512
RWKV_PREP_TILE = 512
RWKV_SCAN_TILE = 512
POOL_TILE = 512
CAST_ROWS = 256

_NT = (((1,), (1,)), ((), ()))


def _rms(x, gain):
    ms = jnp.mean(x * x, axis=-1, keepdims=True)
    return x * lax.rsqrt(ms + RMS_EPS) * gain


def _params(*sem, vmem=VMEM_LIMIT_BYTES):
    return pltpu.CompilerParams(dimension_semantics=sem, vmem_limit_bytes=vmem)


def _ffn_body(h_ref, g_ref, wg_ref, wu_ref, wd_ref, fg_ref, o_ref, n_ref, *, final):
    f = pl.program_id(1)

    @pl.when(f == 0)
    def _():
        h = h_ref[...]
        n_ref[...] = _rms(h, g_ref[...]).astype(BF16)
        o_ref[...] = h

    n = n_ref[...]
    gate = jnp.dot(n, wg_ref[...], preferred_element_type=F32)
    up = jnp.dot(n, wu_ref[...], preferred_element_type=F32)
    act = (gate * jax.nn.sigmoid(gate) * up).astype(BF16)
    o_ref[...] += 0.5 * jnp.dot(act, wd_ref[...], preferred_element_type=F32)

    if final:
        @pl.when(f == pl.num_programs(1) - 1)
        def _():
            o_ref[...] = _rms(o_ref[...], fg_ref[...])


def _ffn(h, gain, wg, wu, wd, final_gain, *, final, tm, tf):
    t, d = h.shape
    f = wg.shape[-1]
    return pl.pallas_call(
        functools.partial(_ffn_body, final=final),
        grid=(t // tm, f // tf),
        in_specs=[
            pl.BlockSpec((tm, d), lambda i, j: (i, 0)),
            pl.BlockSpec((1, d), lambda i, j: (0, 0)),
            pl.BlockSpec((d, tf), lambda i, j: (0, j)),
            pl.BlockSpec((d, tf), lambda i, j: (0, j)),
            pl.BlockSpec((tf, d), lambda i, j: (j, 0)),
            pl.BlockSpec((1, d), lambda i, j: (0, 0)),
        ],
        out_specs=pl.BlockSpec((tm, d), lambda i, j: (i, 0)),
        out_shape=jax.ShapeDtypeStruct((t, d), F32),
        scratch_shapes=[pltpu.VMEM((tm, d), BF16)],
        compiler_params=_params("parallel", "arbitrary", vmem=FFN_VMEM_LIMIT_BYTES),
        name="ffn",
    )(h, gain, wg, wu, wd, final_gain)


def _cast_specs(jobs, steps, flat):
    in_specs, out_specs, out_shapes, args = [], [], [], []
    for w, layer in jobs:
        _, r, c = w.shape
        rows = r // steps
        assert r % steps == 0 and rows % (2 * SUBLANES) == 0
        in_specs.append(pl.BlockSpec((None, rows, c), lambda *ids, layer=layer: (layer, flat(*ids), 0)))
        out_specs.append(pl.BlockSpec((rows, c), lambda *ids: (flat(*ids), 0)))
        out_shapes.append(jax.ShapeDtypeStruct((r, c), BF16))
        args.append(w)
    return in_specs, out_specs, out_shapes, args


def _cast_body(w_ref, o_ref):
    o_ref[...] = w_ref[...].astype(BF16)


def _cast_matrix(w, layer, *, rows):
    _, r, c = w.shape
    assert r % rows == 0 and rows % (2 * SUBLANES) == 0
    return pl.pallas_call(
        _cast_body,
        grid=(r // rows,),
        in_specs=[pl.BlockSpec((None, rows, c), lambda i: (layer, i, 0))],
        out_specs=pl.BlockSpec((rows, c), lambda i: (i, 0)),
        out_shape=jax.ShapeDtypeStruct((r, c), BF16),
        compiler_params=_params("parallel"),
        name="cast",
    )(w)


def _inproj_body(h_ref, g_ref, w_ref, z_ref, n_ref):
    @pl.when(pl.program_id(1) == 0)
    def _():
        n_ref[...] = _rms(h_ref[...], g_ref[...]).astype(BF16)

    z_ref[...] = jnp.dot(n_ref[...], w_ref[...], preferred_element_type=F32)


def _inproj(h, gain, w_in, e, *, tm, tn):
    t, d = h.shape
    n = w_in.shape[-1]
    return pl.pallas_call(
        _inproj_body,
        grid=(t // tm, n // tn),
        in_specs=[
            pl.BlockSpec((tm, d), lambda i, j: (i, 0)),
            pl.BlockSpec((1, d), lambda i, j: (0, 0)),
            pl.BlockSpec((None, d, tn), lambda i, j: (e, 0, j)),
        ],
        out_specs=pl.BlockSpec((tm, tn), lambda i, j: (i, j)),
        out_shape=jax.ShapeDtypeStruct((t, n), F32),
        scratch_shapes=[pltpu.VMEM((tm, d), BF16)],
        compiler_params=_params("parallel", "arbitrary"),
        name="inproj",
    )(h, gain, w_in)


def _split_bf16(x, parts):
    out = []
    for _ in range(parts):
        p = x.astype(BF16)
        out.append(p)
        x = x - p.astype(F32)
    return out


def _nt_dot_3pass(a, b):
    ah, al = _split_bf16(a, 2)
    bh, bl = _split_bf16(b, 2)
    n = a.shape[0]
    both = lax.dot_general(jnp.concatenate([ah, al], axis=0), bh, _NT, preferred_element_type=F32)
    return both[0:n] + both[n:] + lax.dot_general(ah, bl, _NT, preferred_element_type=F32)


def _moba_body(q_ref, k_ref, v_ref, *rest, ncast, hg, nb, blk, topk, scale):
    cast_in, o_ref, cast_out = rest[:ncast], rest[ncast], rest[ncast + 1:2 * ncast + 1]
    kb_ref, vt_ref, km_ref, sel_ref, s0_ref, s1_ref, acc_ref = rest[2 * ncast + 1:]
    for src, dst in zip(cast_in, cast_out):
        dst[...] = src[...].astype(BF16)
    i = pl.program_id(2)
    dh = A_HEAD_DIM
    cols = [slice(hh * dh, (hh + 1) * dh) for hh in range(hg)]

    @pl.when(i == 0)
    def _():
        km_ref[...] = jnp.zeros_like(km_ref)
        for hh in range(hg):
            for jb in range(nb):
                rows = slice(jb * blk, (jb + 1) * blk)
                kf = k_ref[0, rows, cols[hh]]
                kb_ref[hh, jb] = kf.astype(BF16)
                vt_ref[hh, jb] = v_ref[0, rows, cols[hh]].T.astype(BF16)
                km_ref[hh, jb:jb + 1, :] = jnp.mean(kf, axis=0, keepdims=True)

    qs = [q_ref[0, :, c] for c in cols]
    for hh in range(hg):
        gate = _nt_dot_3pass(km_ref[hh], qs[hh])
        blk_id = lax.broadcasted_iota(jnp.int32, gate.shape, 0)
        rank = jnp.zeros(gate.shape, jnp.int32)
        for jp in range(nb):
            row = gate[jp:jp + 1, :]
            beats = (row > gate) | ((row == gate) & (jp < blk_id))
            rank = rank + jnp.where(beats & (jp < i), 1, 0)
        sel_ref[hh] = jnp.where((blk_id < i) & (rank < topk), 0.0, NEG_BIG)

    qb = [(q * (scale * LOG2E)).astype(BF16) for q in qs]

    def scores(hh, j):
        return lax.dot_general(kb_ref[hh, j], qb[hh], _NT, preferred_element_type=F32)

    def pv(hh, p, j):
        return jnp.dot(vt_ref[hh, j], p.astype(BF16), preferred_element_type=F32)

    kpos = lax.broadcasted_iota(jnp.int32, (blk, blk), 0)
    qpos = lax.broadcasted_iota(jnp.int32, (blk, blk), 1)
    causal = kpos <= qpos
    heads = range(hg)
    ss = [jnp.where(causal, scores(hh, i), NEG_BIG) for hh in heads]
    ms = [jnp.max(s, axis=0, keepdims=True) for s in ss]
    ps = [jnp.exp2(s - m) for s, m in zip(ss, ms)]
    ls = [jnp.sum(p, axis=0, keepdims=True) for p in ps]
    accs = [pv(hh, ps[hh], i) for hh in heads]

    for hh in heads:
        acc_ref[hh] = accs[hh]

    def produce(j, s_ref):
        mx = []
        for hh in heads:
            s = scores(hh, jnp.minimum(j, nb - 1))
            s_ref[hh] = s
            mx.append(jnp.max(s, axis=0, keepdims=True))
        return tuple(mx)

    def consume(j, s_ref, ms, ls, mx):
        sel = [sel_ref[hh, pl.ds(j, 1), :] for hh in heads]
        m_new = [jnp.maximum(m, x + b) for m, x, b in zip(ms, mx, sel)]
        alpha = [jnp.exp2(m - mn) for m, mn in zip(ms, m_new)]
        shift = [mn - 2.0 * b for mn, b in zip(m_new, sel)]
        ps = [jnp.exp2(s_ref[hh] - shift[hh]) for hh in heads]
        ls = [a * l + jnp.sum(p, axis=0, keepdims=True) for a, l, p in zip(alpha, ls, ps)]
        pvs = [pv(hh, ps[hh], j) for hh in heads]
        for hh in heads:
            acc_ref[hh] = alpha[hh] * acc_ref[hh] + pvs[hh]
        return tuple(m_new), tuple(ls)

    def past(jj, carry):
        ms, ls, mx0 = carry
        j0 = 2 * jj
        mx1 = produce(j0 + 1, s1_ref)
        ms, ls = consume(j0, s0_ref, ms, ls, mx0)
        mx0 = produce(j0 + 2, s0_ref)
        ms, ls = consume(j0 + 1, s1_ref, ms, ls, mx1)
        return ms, ls, mx0

    mx0 = produce(0, s0_ref)
    _, ls, _ = lax.fori_loop(0, (i + 1) // 2, past, (tuple(ms), tuple(ls), mx0))
    for hh in heads:
        o_ref[0, :, cols[hh]] = (acc_ref[hh] / ls[hh]).T.astype(o_ref.dtype)


def _moba(z, cast_jobs, *, heads, hg):
    b, s, _ = z.shape
    dh, blk = A_HEAD_DIM, MOBA_BLOCK
    nb = s // blk
    nbp = -(-nb // SUBLANES) * SUBLANES
    ng = heads // hg
    assert s % blk == 0 and heads % hg == 0
    c_in, c_out, c_shape, c_args = _cast_specs(cast_jobs, b * ng * nb, lambda bi, g, i: (bi * ng + g) * nb + i)
    outs = pl.pallas_call(
        functools.partial(_moba_body, ncast=len(cast_jobs), hg=hg, nb=nb, blk=blk, topk=min(MOBA_TOPK, nb),
                          scale=dh ** -0.5),
        grid=(b, ng, nb),
        in_specs=[
            pl.BlockSpec((1, blk, hg * dh), lambda bi, g, i: (bi, i, g)),
            pl.BlockSpec((1, s, hg * dh), lambda bi, g, i: (bi, 0, ng + g)),
            pl.BlockSpec((1, s, hg * dh), lambda bi, g, i: (bi, 0, 2 * ng + g)),
        ] + c_in,
        out_specs=[pl.BlockSpec((1, blk, hg * dh), lambda bi, g, i: (bi, i, g))] + c_out,
        out_shape=[jax.ShapeDtypeStruct((b, s, heads * dh), BF16)] + c_shape,
        scratch_shapes=[pltpu.VMEM((hg, nb, blk, dh), BF16), pltpu.VMEM((hg, nb, dh, blk), BF16),
                        pltpu.VMEM((hg, nbp, dh), F32), pltpu.VMEM((hg, nbp, blk), F32),
                        pltpu.VMEM((hg, blk, blk), F32), pltpu.VMEM((hg, blk, blk), F32),
                        pltpu.VMEM((hg, dh, blk), F32)],
        compiler_params=_params("parallel", "parallel", "arbitrary"),
        name="moba",
    )(z, z, z, *c_args)
    return outs[0], outs[1:]


def _split_dot(x, ones_bd):
    hi = x.astype(BF16)
    lo = (x - hi.astype(F32)).astype(BF16)
    return (jnp.dot(hi, ones_bd, preferred_element_type=F32)
            + jnp.dot(lo, ones_bd, preferred_element_type=F32))


def _head_sum(x, ones_bd):
    cols = [_split_dot(x[:, c:c + LANES], ones_bd) for c in range(0, x.shape[-1], LANES)]
    return cols[0] if len(cols) == 1 else jnp.concatenate(cols, axis=-1)


def _head_ones():
    r = lax.broadcasted_iota(jnp.int32, (LANES, LANES), 0) // B_HEAD_DIM
    c = lax.broadcasted_iota(jnp.int32, (LANES, LANES), 1) // B_HEAD_DIM
    return jnp.where(r == c, 1.0, 0.0).astype(BF16)


def _prep_body(zr_ref, zk_ref, zv_ref, zl_ref, pr_ref, pk_ref, pv_ref, plo_ref,
               mur_ref, muk_ref, muv_ref, mul_ref, w0_ref, a0_ref, kk_ref, ka_ref, rk_ref,
               w2_ref, a2_ref, g2_ref, tri_ref, ones_ref,
               at_out, rp_out, bm_out, km_out, bc_out, kc_out, v_out, pc_out, bonus_out, g_out):
    ts = zr_ref.shape[1]
    first = pl.program_id(1) == 0
    row8 = lax.broadcasted_iota(jnp.int32, (SUBLANES, 1), 0)

    def lerp(z_ref, prev_ref, mu_ref):
        x = z_ref[0]
        prev = jnp.where(first, 0.0, prev_ref[0, SUBLANES - 1:SUBLANES, :])
        rolled = pltpu.roll(x, 1, axis=0)
        head = jnp.where(row8 == 0, prev, rolled[0:SUBLANES])
        shifted = jnp.concatenate([head, rolled[SUBLANES:]], axis=0)
        return x + (shifted - x) * mu_ref[...]

    r = lerp(zr_ref, pr_ref, mur_ref)
    k = lerp(zk_ref, pk_ref, muk_ref)
    v = lerp(zv_ref, pv_ref, muv_ref)
    lo = lerp(zl_ref, plo_ref, mul_ref)
    wa = lo[:, 0:LANES]
    gl = lo[:, LANES:2 * LANES]

    w_in = w0_ref[...] + jnp.dot(jnp.tanh(wa).astype(BF16), w2_ref[...], preferred_element_type=F32)
    lw = -math.exp(-0.5) * jax.nn.sigmoid(w_in)
    a = jax.nn.sigmoid(a0_ref[...] + jnp.dot(wa.astype(BF16), a2_ref[...], preferred_element_type=F32))
    g = jnp.dot(jax.nn.sigmoid(gl).astype(BF16), g2_ref[...], preferred_element_type=F32)

    ones_bd = ones_ref[...]
    kk = k * kk_ref[...]
    kk = kk * lax.rsqrt(jnp.maximum(_head_sum(kk * kk, ones_bd), 1e-24))
    k2 = k * (1.0 + (a - 1.0) * ka_ref[...])
    bonus = _head_sum(r * k2 * rk_ref[...], ones_bd) * v

    c_len = RWKV_CHUNK
    nc = ts // c_len
    bw = lw.shape[1]
    lw2 = lw * LOG2E
    y = jnp.dot(tri_ref[...], jnp.concatenate(_split_bf16(lw2, 3), axis=1), preferred_element_type=F32)
    cum = y[:, 0:bw] + y[:, bw:2 * bw] + y[:, 2 * bw:]
    last = cum.reshape(nc, c_len, bw)[:, c_len - 1:c_len, :]
    total = jnp.broadcast_to(last, (nc, c_len, bw)).reshape(ts, bw)
    p_inv = jnp.exp2(-cum)
    to_end = jnp.exp2(total - cum)
    b = kk * a
    at_out[0] = (-kk * jnp.exp2(cum - lw2)).astype(BF16)
    rp_out[0] = (r * jnp.exp2(cum)).astype(BF16)
    bm_out[0] = (b * p_inv).astype(BF16)
    km_out[0] = (k2 * p_inv).astype(BF16)
    bc_out[0] = (b * to_end).astype(BF16)
    kc_out[0] = (k2 * to_end).astype(BF16)
    v_out[0] = v.astype(BF16)
    pc_out[0] = jnp.exp2(jnp.broadcast_to(last, (nc, SUBLANES, bw)))
    bonus_out[0] = bonus
    g_out[0] = g


def _rwkv_prep(z, p, *, zb_col, ts):
    b, s, _ = z.shape
    bw = p["w0"].shape[-1]
    lw = 2 * LANES
    cb = zb_col // bw
    lb = (zb_col + 3 * bw) // lw
    assert zb_col % bw == 0 and (zb_col + 3 * bw) % lw == 0 and s % ts == 0 and ts % SUBLANES == 0
    sub = ts // SUBLANES

    def cur(col, width):
        return pl.BlockSpec((1, ts, width), lambda bi, t: (bi, t, col))

    def prev(col, width):
        return pl.BlockSpec((1, SUBLANES, width), lambda bi, t: (bi, jnp.maximum(t * sub - 1, 0), col))

    def vec(width):
        return pl.BlockSpec((1, width), lambda bi, t: (0, 0))

    def mat(width):
        return pl.BlockSpec((LANES, width), lambda bi, t: (0, 0))

    assert ts % RWKV_CHUNK == 0
    nc = ts // RWKV_CHUNK
    ti = jnp.arange(ts)
    tri = ((ti[:, None] // RWKV_CHUNK == ti[None, :] // RWKV_CHUNK) & (ti[None, :] <= ti[:, None])).astype(BF16)
    li = jnp.arange(LANES)
    ones_bd = (li[:, None] // B_HEAD_DIM == li[None, :] // B_HEAD_DIM).astype(BF16)

    def whole(shape):
        return pl.BlockSpec(shape, lambda bi, t: (0, 0))

    out_spec = pl.BlockSpec((1, ts, bw), lambda bi, t: (bi, t, 0))
    pc_spec = pl.BlockSpec((1, nc, SUBLANES, bw), lambda bi, t: (bi, t, 0, 0))
    tok16 = jax.ShapeDtypeStruct((b, s, bw), BF16)
    tok32 = jax.ShapeDtypeStruct((b, s, bw), F32)
    pc_shape = jax.ShapeDtypeStruct((b, s // RWKV_CHUNK, SUBLANES, bw), F32)
    return pl.pallas_call(
        _prep_body,
        grid=(b, s // ts),
        in_specs=[cur(cb, bw), cur(cb + 1, bw), cur(cb + 2, bw), cur(lb, lw),
                  prev(cb, bw), prev(cb + 1, bw), prev(cb + 2, bw), prev(lb, lw),
                  vec(bw), vec(bw), vec(bw), vec(lw), vec(bw), vec(bw), vec(bw), vec(bw), vec(bw),
                  mat(bw), mat(bw), mat(bw),
                  whole((ts, ts)), whole((LANES, LANES))],
        out_specs=[out_spec] * 7 + [pc_spec, out_spec, out_spec],
        out_shape=[tok16] * 7 + [pc_shape, tok32, tok32],
        compiler_params=_params("parallel", "arbitrary"),
        name="rwkv_prep",
    )(z, z, z, z, z, z, z, z,
      p["mu_r"], p["mu_k"], p["mu_v"], p["mu_l"], p["w0"], p["a0"], p["k_k"], p["k_a"], p["r_k"],
      p["w2"], p["a2"], p["g2"], tri, ones_bd)


def _bdot(a, b):
    return jnp.dot(a.astype(BF16), b.astype(BF16), preferred_element_type=F32)


def _scan_body(at_ref, rp_ref, bm_ref, km_ref, bc_ref, kc_ref, v_ref, pc_ref, bonus_ref, g_ref, gng_ref, gnb_ref,
               *rest, ncast, chunks, pairs, unroll):
    cast_in, o_ref, cast_out, h_ref = rest[:ncast], rest[ncast], rest[ncast + 1:2 * ncast + 1], rest[2 * ncast + 1]
    for src, dst in zip(cast_in, cast_out):
        dst[...] = src[...].astype(BF16)
    c_len = RWKV_CHUNK
    hd = B_HEAD_DIM
    prs = range(pairs)
    lanes = [slice(p * LANES, (p + 1) * LANES) for p in prs]

    @pl.when(pl.program_id(1) == 0)
    def _():
        h_ref[...] = jnp.zeros_like(h_ref)

    lane = lax.broadcasted_iota(jnp.int32, (c_len, LANES), 1)
    rowc = lax.broadcasted_iota(jnp.int32, (c_len, LANES), 0)
    head0 = lane < hd
    scol = lane % hd
    strict = scol < rowc
    incl = scol <= rowc
    r2 = lax.broadcasted_iota(jnp.int32, (LANES, LANES), 0)
    c2 = lax.broadcasted_iota(jnp.int32, (LANES, LANES), 1)
    bd_mask = (r2 // hd) == (c2 // hd)
    eye_cat = jnp.where(scol == rowc, 1.0, 0.0)
    ones_bd = _head_ones()
    gng = [gng_ref[:, ln] for ln in lanes]
    gnb = [gnb_ref[:, ln] for ln in lanes]

    def stack(x):
        return jnp.concatenate([jnp.where(head0, x, 0.0), jnp.where(head0, 0.0, x)], axis=0)

    def chunk_step(c, carry):
        rows = [pl.ds(pl.multiple_of((c * unroll + ci) * c_len, c_len), c_len) for ci in range(unroll)]
        units = [(ci, p) for ci in range(unroll) for p in prs]
        us = range(len(units))

        def load(ref):
            return [ref[0, rows[ci], lanes[p]] for ci, p in units]

        at, rp, bm, km, v = load(at_ref), load(rp_ref), load(bm_ref), load(km_ref), load(v_ref)
        bc_t = [x.astype(F32).T for x in load(bc_ref)]
        kc_t = [x.astype(F32).T for x in load(kc_ref)]
        pc_col = [pc_ref[0, c * unroll + ci, :, lanes[p]].T[:, 0:1] for ci, p in units]

        lhs = [jnp.concatenate([at[u], rp[u]], axis=0) for u in us]
        rhs = [jnp.concatenate([stack(bm[u]), stack(km[u])], axis=0) for u in us]
        amat = [lax.dot_general(lhs[u], rhs[u], _NT, preferred_element_type=F32) for u in us]
        a_ab = [jnp.where(strict, m[0:c_len, 0:LANES], 0.0) for m in amat]
        a_ak = [jnp.where(strict, m[0:c_len, LANES:2 * LANES], 0.0) for m in amat]
        a_rb = [jnp.where(incl, m[c_len:, 0:LANES], 0.0) for m in amat]
        a_rk = [jnp.where(incl, m[c_len:, LANES:2 * LANES], 0.0) for m in amat]

        pw = a_ab
        t_inv = [eye_cat + x for x in a_ab]
        pw_bd = [stack(x) for x in pw]
        pw = [_bdot(pw[u], pw_bd[u]) for u in us]
        for _ in range(int(math.log2(c_len)) - 2):
            pw_bd = [stack(x) for x in pw]
            both = [_bdot(jnp.concatenate([pw[u], t_inv[u]], axis=0), pw_bd[u]) for u in us]
            pw = [x[0:c_len] for x in both]
            t_inv = [t_inv[u] + both[u][c_len:] for u in us]
        pw_bd = [stack(x) for x in pw]
        t_inv = [t_inv[u] + _bdot(t_inv[u], pw_bd[u]) for u in us]

        v_st = [stack(x) for x in v]
        akv = [_bdot(a_ak[u], v_st[u]) for u in us]
        sol = [_bdot(t_inv[u], jnp.concatenate([stack(at[u]), stack(akv[u])], axis=1)) for u in us]
        y0 = [_bdot(a_rk[u], v_st[u]) for u in us]
        hkv = [jnp.where(bd_mask, _bdot(kc_t[u], v[u]), 0.0) for u in us]

        h = [h_ref[p] for p in prs]
        ys = []
        for ci in range(unroll):
            un = [ci * pairs + p for p in prs]
            hb = [x.astype(BF16) for x in h]
            uu = [jnp.dot(sol[un[p]][:, 0:LANES].astype(BF16), hb[p], preferred_element_type=F32)
                  + sol[un[p]][:, LANES:] for p in prs]
            y = [jnp.dot(rp[un[p]].astype(BF16), hb[p], preferred_element_type=F32) + y0[un[p]] for p in prs]
            ys += [y[p] + _bdot(a_rb[un[p]], stack(uu[p])) for p in prs]
            h = [pc_col[un[p]] * h[p] + jnp.where(bd_mask, _bdot(bc_t[un[p]], uu[p]), 0.0) + hkv[un[p]]
                 for p in prs]
        for p in prs:
            h_ref[p] = h[p]

        mean = [_split_dot(x, ones_bd) * (1.0 / hd) for x in ys]
        dlt = [ys[u] - mean[u] for u in us]
        var = [_split_dot(x * x, ones_bd) * (1.0 / hd) for x in dlt]
        for u, (ci, p) in enumerate(units):
            yn = dlt[u] * lax.rsqrt(var[u] + GN_EPS) * gng[p] + gnb[p]
            out = (yn + bonus_ref[0, rows[ci], lanes[p]]) * g_ref[0, rows[ci], lanes[p]]
            o_ref[0, rows[ci], lanes[p]] = out.astype(o_ref.dtype)
        return carry

    lax.fori_loop(0, chunks // unroll, chunk_step, 0)


def _rwkv_scan(prep, gn_g, gn_b, cast_jobs, *, ts):
    b, s, bw = prep[0].shape
    assert ts % (RWKV_CHUNK * RWKV_UNROLL) == 0 and s % ts == 0 and bw % LANES == 0
    pairs = bw // LANES
    tok = pl.BlockSpec((1, ts, bw), lambda bi, t: (bi, t, 0))
    pcs = pl.BlockSpec((1, ts // RWKV_CHUNK, SUBLANES, bw), lambda bi, t: (bi, t, 0, 0))
    vec = pl.BlockSpec((1, bw), lambda bi, t: (0, 0))
    nt = s // ts
    c_in, c_out, c_shape, c_args = _cast_specs(cast_jobs, b * nt, lambda bi, t: bi * nt + t)
    outs = pl.pallas_call(
        functools.partial(_scan_body, ncast=len(cast_jobs), chunks=ts // RWKV_CHUNK, pairs=pairs,
                          unroll=RWKV_UNROLL),
        grid=(b, nt),
        in_specs=[tok] * 7 + [pcs, tok, tok, vec, vec] + c_in,
        out_specs=[tok] + c_out,
        out_shape=[jax.ShapeDtypeStruct((b, s, bw), BF16)] + c_shape,
        scratch_shapes=[pltpu.VMEM((pairs, LANES, LANES), F32)],
        compiler_params=_params("parallel", "arbitrary"),
        name="rwkv_scan",
    )(*prep, gn_g, gn_b, *c_args)
    return outs[0], outs[1:]


def _outproj_body(h_ref, ya_ref, yb_ref, wa_ref, wb_ref, o_ref):
    acc = jnp.dot(ya_ref[...].astype(BF16), wa_ref[...], preferred_element_type=F32)
    acc = acc + jnp.dot(yb_ref[...].astype(BF16), wb_ref[...], preferred_element_type=F32)
    o_ref[...] = h_ref[...] + acc


def _outproj(h, ya, yb, w_out, e, *, tm, tn):
    t, d = h.shape
    wa, wb = ya.shape[-1], yb.shape[-1]
    assert wa == wb
    return pl.pallas_call(
        _outproj_body,
        grid=(t // tm, d // tn),
        in_specs=[
            pl.BlockSpec((tm, tn), lambda i, j: (i, j)),
            pl.BlockSpec((tm, wa), lambda i, j: (i, 0)),
            pl.BlockSpec((tm, wb), lambda i, j: (i, 0)),
            pl.BlockSpec((None, wa, tn), lambda i, j: (e, 0, j)),
            pl.BlockSpec((None, wb, tn), lambda i, j: (e, 1, j)),
        ],
        out_specs=pl.BlockSpec((tm, tn), lambda i, j: (i, j)),
        out_shape=jax.ShapeDtypeStruct((t, d), F32),
        compiler_params=_params("parallel", "arbitrary"),
        name="outproj",
    )(h, ya, yb, w_out, w_out)


def _pool_body(h_ref, halo_ref, g_ref, w_ref, sc_ref, o_ref, *, windows, halo):
    ts, d = h_ref.shape[1], h_ref.shape[2]
    grp = d // len(windows)
    t = pl.program_id(1)
    hcur = h_ref[0]
    u = _rms(hcur, g_ref[...])
    uh = jnp.where(t == 0, 0.0, _rms(halo_ref[0], g_ref[...]))
    pos = t * ts + lax.broadcasted_iota(jnp.int32, (ts, 1), 0)
    outs = []
    for gi, win in enumerate(windows):
        cols = slice(gi * grp, (gi + 1) * grp)
        ug = u[:, cols]
        ext = jnp.concatenate([uh[:, cols], ug], axis=0)
        span = 1
        while span < win:
            ext = ext + pltpu.roll(ext, span, axis=0)
            span *= 2
        cnt = jnp.minimum(pos + 1, win).astype(F32)
        diff = ext[halo:, :] / cnt - ug
        outs.append(jnp.dot(diff.astype(BF16), w_ref[gi], preferred_element_type=F32))
    o_ref[0] = hcur + jnp.concatenate(outs, axis=-1) * sc_ref[...]


def _pool(h, gain, w_grp, o, scale, *, ts):
    b, s, d = h.shape
    halo = 2 * SUBLANES
    assert max(POOL_WINDOWS) <= halo and all(w & (w - 1) == 0 for w in POOL_WINDOWS)
    assert s % ts == 0 and ts % halo == 0
    ng, grp = w_grp.shape[1], w_grp.shape[2]
    sub = ts // halo
    return pl.pallas_call(
        functools.partial(_pool_body, windows=POOL_WINDOWS, halo=halo),
        grid=(b, s // ts),
        in_specs=[
            pl.BlockSpec((1, ts, d), lambda bi, t: (bi, t, 0)),
            pl.BlockSpec((1, halo, d), lambda bi, t: (bi, jnp.maximum(t * sub - 1, 0), 0)),
            pl.BlockSpec((1, d), lambda bi, t: (0, 0)),
            pl.BlockSpec((None, ng, grp, grp), lambda bi, t: (o, 0, 0, 0)),
            pl.BlockSpec((1, d), lambda bi, t: (0, 0)),
        ],
        out_specs=pl.BlockSpec((1, ts, d), lambda bi, t: (bi, t, 0)),
        out_shape=jax.ShapeDtypeStruct((b, s, d), F32),
        compiler_params=_params("parallel", "arbitrary"),
        name="pool",
    )(h, h, gain, w_grp, scale)


def _pick(n, pref):
    for c in range(min(pref, n), 0, -1):
        if n % c == 0 and (c % SUBLANES == 0 or c == n):
            return c
    return n


def kernel(x, ffn1_norm, ffn1_wg, ffn1_wu, ffn1_wd, mix_norm, ffn2_norm, ffn2_wg, ffn2_wu, ffn2_wd,
           ab_w_in, ab_w_out, rwkv_mu, rwkv_w0, rwkv_w2, rwkv_a0, rwkv_a2, rwkv_g2, rwkv_k_k, rwkv_k_a,
           rwkv_r_k, rwkv_gn_g, rwkv_gn_b, pool_w, pool_scale, final_norm):
    b, s, d = x.shape
    t = b * s
    depth = ffn1_norm.shape[0]
    bw = rwkv_w0.shape[-1]
    aw = ab_w_out.shape[1] - bw
    a_heads = aw // A_HEAD_DIM
    zb_col = 3 * aw
    assert rwkv_w2.shape[1] == LORA and 3 * LORA <= 2 * LANES

    ffn_w = [(ffn1_wg, ffn1_wu, ffn1_wd), (ffn2_wg, ffn2_wu, ffn2_wd)]
    n_ffn = 2 * depth
    ffn_b = {n: [None] * 3 for n in range(n_ffn)}
    ffn_b[0] = [_cast_matrix(w, 0, rows=CAST_ROWS) for w in ffn_w[0]]

    def cast_jobs(items):
        return [(ffn_w[n % 2][k], n // 2) for n, k in items]

    def casted(items, outs):
        for (n, k), w in zip(items, outs):
            ffn_b[n][k] = w

    n_in = ab_w_in.shape[-1]
    n_pad = zb_col + 3 * bw + 2 * LANES - n_in
    w_in = jnp.pad(ab_w_in, ((0, 0), (0, 0), (0, n_pad))).astype(BF16)
    w_out = ab_w_out.astype(BF16)
    pool_wb = pool_w.astype(BF16)
    zrow = jnp.zeros((LANES - LORA, bw), F32)

    tm = _pick(t, TOKEN_TILE)
    tmo = _pick(t, OUTPROJ_TOKEN_TILE)
    tf = _pick(ffn1_wg.shape[-1], FF_TILE)
    row = lambda v: v.reshape(1, -1)

    h = x.reshape(t, d)
    for layer in range(depth):
        h = _ffn(h, row(ffn1_norm[layer]), *ffn_b[2 * layer], row(final_norm), final=False, tm=tm, tf=tf)
        if layer % 2 == 0:
            e = layer // 2
            z = _inproj(h, row(mix_norm[layer]), w_in, e, tm=tm, tn=_pick(w_in.shape[-1], INPROJ_COL_TILE))
            z = z.reshape(b, s, -1)
            todo = list(range(2 * layer + 1, min(2 * layer + 5, n_ffn)))
            ns_a = [(n, k) for n in todo[:3] for k in (0, 1)]
            ns_b = [(n, k) for n in todo[3:] for k in (0, 1)] + [(n, 2) for n in todo]
            ya, c_a = _moba(z, cast_jobs(ns_a), heads=a_heads, hg=math.gcd(a_heads, MOBA_HEAD_GROUP))
            casted(ns_a, c_a)
            mu = rwkv_mu[e]
            prm = {
                "mu_r": row(mu[0:bw]), "mu_k": row(mu[bw:2 * bw]), "mu_v": row(mu[2 * bw:3 * bw]),
                "mu_l": row(jnp.pad(mu[3 * bw:], (0, n_pad))),
                "w0": row(rwkv_w0[e]), "a0": row(rwkv_a0[e]), "k_k": row(rwkv_k_k[e]),
                "k_a": row(rwkv_k_a[e]), "r_k": row(rwkv_r_k[e]),
                "w2": jnp.concatenate([rwkv_w2[e], zrow], axis=0).astype(BF16),
                "a2": jnp.concatenate([zrow, rwkv_a2[e]], axis=0).astype(BF16),
                "g2": jnp.concatenate([rwkv_g2[e], zrow], axis=0).astype(BF16),
            }
            prep = _rwkv_prep(z, prm, zb_col=zb_col, ts=_pick(s, RWKV_PREP_TILE))
            yb, c_b = _rwkv_scan(prep, row(rwkv_gn_g[e]), row(rwkv_gn_b[e]), cast_jobs(ns_b),
                                 ts=_pick(s, RWKV_SCAN_TILE))
            casted(ns_b, c_b)
            h = _outproj(h, ya.reshape(t, aw), yb.reshape(t, bw), w_out, e, tm=tmo, tn=d)
        else:
            o = layer // 2
            h = _pool(h.reshape(b, s, d), row(mix_norm[layer]), pool_wb, o, row(pool_scale[o]),
                      ts=_pick(s, POOL_TILE)).reshape(t, d)
        h = _ffn(h, row(ffn2_norm[layer]), *ffn_b[2 * layer + 1], row(final_norm),
                 final=(layer == depth - 1), tm=tm, tf=tf)
    return h.reshape(b, s, d)
```
